```python
import math
import jax, jax.numpy as jnp
from jax import lax
import numpy as np

D_MODEL = 1024
BATCH = 4
SEQ = 4096
DEPTH = 1

GRID_W = 64
CTX_LEN = 256
NORM_EPS = 1e-6
ATT_HEADS = 8
ATT_DK = 64
ATT_DV = 2 * ATT_DK
ATT_W = ATT_HEADS * ATT_DV
ROPE_BASE = 10000.0
Q_BLOCK = 128
SSM_GROUP = 16
SSM_W = D_MODEL // 2
SSM_GROUPS = SSM_W // SSM_GROUP
SSM_STATE = 64
DT_MIN = 1e-3
DT_MAX = 1e-1
PEER_HEADS = 8
PEER_NKEYS = 128
PEER_EXPERTS = PEER_NKEYS * PEER_NKEYS
PEER_TOPK = 16
PEER_DKEY = 128
PEER_BLOCK = 128
N_Q = 2 * ATT_HEADS * ATT_DK
N_K = 2 * ATT_HEADS * ATT_DK
N_V = ATT_W
N_IN_MIX = N_Q + N_K + N_V + SSM_W
N_IN = N_IN_MIX + 2 * D_MODEL

kernel_name = 'hybrid_diffattn_s5_peer_dit_block'


def _rmsnorm(h, g):
    hf = h.astype(jnp.float32)
    hf = hf * lax.rsqrt(jnp.mean(hf * hf, axis=-1, keepdims=True) + NORM_EPS)
    return (hf * g.astype(jnp.float32)).astype(h.dtype)


def _modulate(h, shift, scale):
    return h * (1 + scale) + shift


def _axial_angles(pos):
    n_freq = ATT_DK // 4
    inv = ROPE_BASE ** (-2.0 * jnp.arange(n_freq, dtype=jnp.float32) / (ATT_DK // 2))
    return pos.astype(jnp.float32)[:, None] * inv[None, :]


def _rope_1d(h, ang):
    f = ang.shape[-1]
    cos = jnp.cos(ang)[:, None, None, :]
    sin = jnp.sin(ang)[:, None, None, :]
    h1, h2 = h[..., :f], h[..., f:]
    return jnp.concatenate([h1 * cos - h2 * sin, h2 * cos + h1 * sin], axis=-1)


def _axial_rope(h, ang_r, ang_c):
    half = ATT_DK // 2
    hf = h.astype(jnp.float32)
    out = jnp.concatenate([_rope_1d(hf[..., :half], ang_r), _rope_1d(hf[..., half:], ang_c)], axis=-1)
    return out.astype(h.dtype)


def _diff_attend(q, k, v, lam):
    s = jnp.einsum('bqhmd,bkhmd->mbhqk', q, k, preferred_element_type=jnp.float32) * (ATT_DK ** -0.5)
    p = jax.nn.softmax(s, axis=-1)
    w = p[0] - lam * p[1]
    return jnp.einsum('bhqk,bkhe->bqhe', w.astype(v.dtype), v)


def _diff_attend_blocked(q, k, v, lam):
    b, l = q.shape[:2]
    nb = l // Q_BLOCK
    qb = jnp.moveaxis(q.reshape(b, nb, Q_BLOCK, ATT_HEADS, 2, ATT_DK), 1, 0)
    o = lax.map(lambda qblk: _diff_attend(qblk, k, v, lam), qb)
    return jnp.moveaxis(o, 0, 1).reshape(b, l, ATT_HEADS, ATT_DV)


def _diff_post(o, g, lam_init):
    b, l = o.shape[:2]
    return (_rmsnorm(o, g) * (1.0 - lam_init)).reshape(b, l, ATT_W)


def _s5_discretize(a_re, a_im, log_dt, b_re, b_im):
    ar = a_re.astype(jnp.float32)
    ai = a_im.astype(jnp.float32)
    dt = jnp.exp(log_dt.astype(jnp.float32))[:, None]
    mag = jnp.exp(ar * dt)
    abar_re = mag * jnp.cos(ai * dt)
    abar_im = mag * jnp.sin(ai * dt)
    den = ar * ar + ai * ai
    nr = abar_re - 1.0
    ni = abar_im
    fr = (nr * ar + ni * ai) / den
    fi = (ni * ar - nr * ai) / den
    br = b_re.astype(jnp.float32)
    bi = b_im.astype(jnp.float32)
    bbar_re = fr[..., None] * br - fi[..., None] * bi
    bbar_im = fr[..., None] * bi + fi[..., None] * br
    return abar_re, abar_im, bbar_re, bbar_im


def _cplx_combine(e1, e2):
    a1r, a1i, b1r, b1i = e1
    a2r, a2i, b2r, b2i = e2
    return (a2r * a1r - a2i * a1i,
            a2r * a1i + a2i * a1r,
            a2r * b1r - a2i * b1i + b2r,
            a2r * b1i + a2i * b1r + b2i)


def _s5_states(u, disc, h0, reverse):
    abar_re, abar_im, bbar_re, bbar_im = disc
    bu_re = jnp.einsum('gnp,blgp->blgn', bbar_re, u)
    bu_im = jnp.einsum('gnp,blgp->blgn', bbar_im, u)
    if h0 is not None:
        h0_re, h0_im = h0
        first = -1 if reverse else 0
        bu_re = bu_re.at[:, first].add(abar_re * h0_re - abar_im * h0_im)
        bu_im = bu_im.at[:, first].add(abar_re * h0_im + abar_im * h0_re)
    a_re = jnp.broadcast_to(abar_re, bu_re.shape)
    a_im = jnp.broadcast_to(abar_im, bu_im.shape)
    _, _, h_re, h_im = lax.associative_scan(_cplx_combine, (a_re, a_im, bu_re, bu_im), reverse=reverse, axis=1)
    return h_re, h_im


def _s5_readout(c_re, c_im, h):
    return (jnp.einsum('gpn,blgn->blgp', c_re.astype(jnp.float32), h[0])
            - jnp.einsum('gpn,blgn->blgp', c_im.astype(jnp.float32), h[1]))


def _s5_output(u, h_f, h_b, c_re, c_im, d_skip, w_glu, b_glu, dtype):
    b, l = u.shape[:2]
    y = (_s5_readout(c_re[0], c_im[0], h_f) + _s5_readout(c_re[1], c_im[1], h_b)
         + d_skip.astype(jnp.float32).reshape(SSM_GROUPS, SSM_GROUP) * u)
    y = jax.nn.gelu(y.reshape(b, l, SSM_W)).astype(dtype)
    return y * jax.nn.sigmoid(y @ w_glu + b_glu)


def _merge(attn, ssm, gates, w_attn_up, w_ssm_up, w_out):
    g_a, g_s = jnp.split(jax.nn.sigmoid(gates), 2, axis=-1)
    return (g_a * (attn @ w_attn_up) + g_s * (ssm @ w_ssm_up)) @ w_out


def _peer(h, w_query, sub_k1, sub_k2, table_u, table_v):
    b, l, d = h.shape
    half = PEER_DKEY // 2
    n_cand = PEER_TOPK * PEER_TOPK

    def block(tb):
        q = (tb @ w_query).reshape(PEER_BLOCK, PEER_HEADS, PEER_DKEY)
        s1 = jnp.einsum('thd,kd->thk', q[..., :half], sub_k1, preferred_element_type=jnp.float32)
        s2 = jnp.einsum('thd,kd->thk', q[..., half:], sub_k2, preferred_element_type=jnp.float32)
        v1, i1 = lax.top_k(s1, PEER_TOPK)
        v2, i2 = lax.top_k(s2, PEER_TOPK)
        cand_s = (v1[..., :, None] + v2[..., None, :]).reshape(PEER_BLOCK, PEER_HEADS, n_cand)
        cand_i = (i1[..., :, None] * PEER_NKEYS + i2[..., None, :]).reshape(PEER_BLOCK, PEER_HEADS, n_cand)
        top_s, top_j = lax.top_k(cand_s, PEER_TOPK)
        idx = jnp.take_along_axis(cand_i, top_j, axis=-1)
        gate = jax.nn.softmax(top_s, axis=-1)
        u_sel = jnp.take(table_u, idx, axis=0)
        act = jax.nn.gelu(jnp.einsum('thkd,td->thk', u_sel, tb, preferred_element_type=jnp.float32))
        v_sel = jnp.take(table_v, idx, axis=0)
        return jnp.einsum('thk,thkd->td', (gate * act).astype(tb.dtype), v_sel)

    out = lax.map(block, h.reshape(b * l // PEER_BLOCK, PEER_BLOCK, d))
    return out.reshape(b, l, d)


def setup_inputs(seed: int = 0) -> dict:
    key = jax.random.key(seed)
    keys = iter(jax.random.split(key, 48))

    def nrm(shape, scale):
        return jax.random.normal(next(keys), shape, jnp.float32) * scale

    D = D_MODEL
    G, N, P = SSM_GROUPS, SSM_STATE, SSM_GROUP
    half = PEER_DKEY // 2
    a_im_init = math.pi * jnp.arange(N, dtype=jnp.float32)
    return {
        'x': nrm((BATCH, SEQ, D), 1.0),
        'c': nrm((BATCH, D), 1.0),
        'ctx': nrm((BATCH, CTX_LEN, D), 1.0),
        'c_ctx': nrm((D,), 1.0),
        'ada_w': nrm((DEPTH, D, 6 * D), 0.5 * D ** -0.5),
        'ada_b': nrm((DEPTH, 6 * D), 0.02),
        'norm1_g': 1.0 + nrm((DEPTH, D), 0.02),
        'norm2_g': 1.0 + nrm((DEPTH, D), 0.02),
        'w_in': nrm((DEPTH, D, N_IN), D ** -0.5),
        'lambda_q1': nrm((DEPTH, ATT_DK), 0.1),
        'lambda_k1': nrm((DEPTH, ATT_DK), 0.1),
        'lambda_q2': nrm((DEPTH, ATT_DK), 0.1),
        'lambda_k2': nrm((DEPTH, ATT_DK), 0.1),
        'subln_g': 1.0 + nrm((DEPTH, ATT_DV), 0.02),
        'w_attn_up': nrm((DEPTH, ATT_W, D), ATT_W ** -0.5),
        'ssm_a_re': -0.5 + nrm((DEPTH, 2, G, N), 0.01),
        'ssm_a_im': a_im_init + nrm((DEPTH, 2, G, N), 0.01),
        'ssm_log_dt': jax.random.uniform(next(keys), (DEPTH, 2, G), jnp.float32, math.log(DT_MIN), math.log(DT_MAX)),
        'ssm_b_re': nrm((DEPTH, 2, G, N, P), (2 * P) ** -0.5),
        'ssm_b_im': nrm((DEPTH, 2, G, N, P), (2 * P) ** -0.5),
        'ssm_c_re': nrm((DEPTH, 2, G, P, N), (2 * N) ** -0.5),
        'ssm_c_im': nrm((DEPTH, 2, G, P, N), (2 * N) ** -0.5),
        'ssm_d': nrm((DEPTH, SSM_W), 1.0),
        'w_glu': nrm((DEPTH, SSM_W, SSM_W), SSM_W ** -0.5),
        'b_glu': nrm((DEPTH, SSM_W), 0.01),
        'w_ssm_up': nrm((DEPTH, SSM_W, D), SSM_W ** -0.5),
        'w_out': nrm((DEPTH, D, D), D ** -0.5),
        'peer_w_query': nrm((DEPTH, D, PEER_HEADS * PEER_DKEY), D ** -0.5),
        'peer_sub_k1': nrm((DEPTH, PEER_NKEYS, half), half ** -0.5),
        'peer_sub_k2': nrm((DEPTH, PEER_NKEYS, half), half ** -0.5),
        'peer_u': nrm((DEPTH, PEER_EXPERTS, D), D ** -0.5),
        'peer_v': nrm((DEPTH, PEER_EXPERTS, D), 1.0),
        'final_norm_g': 1.0 + nrm((D,), 0.02),
    }


def reference(x, c, ctx, c_ctx, ada_w, ada_b, norm1_g, norm2_g, w_in,
              lambda_q1, lambda_k1, lambda_q2, lambda_k2, subln_g, w_attn_up,
              ssm_a_re, ssm_a_im, ssm_log_dt, ssm_b_re, ssm_b_im, ssm_c_re, ssm_c_im,
              ssm_d, w_glu, b_glu, w_ssm_up, w_out,
              peer_w_query, peer_sub_k1, peer_sub_k2, peer_u, peer_v, final_norm_g):
    b, l, _ = x.shape
    n_ctx = ctx.shape[1]
    ROWS = l // GRID_W
    pos_row = jnp.repeat(jnp.arange(ROWS, dtype=jnp.int32), GRID_W)
    pos_col = jnp.tile(jnp.arange(GRID_W, dtype=jnp.int32), ROWS)
    ang_r = _axial_angles(pos_row)
    ang_c = _axial_angles(pos_col)
    c_act = jax.nn.silu(c)
    cc_act = jax.nn.silu(c_ctx)
    s_q, s_k, s_v = N_Q, N_Q + N_K, N_Q + N_K + N_V

    for i in range(DEPTH):
        ctx_out = i + 1 < DEPTH
        sh1, sc1, g1, sh2, sc2, g2 = jnp.split((c_act @ ada_w[i] + ada_b[i])[:, None, :], 6, axis=-1)
        sh1c, sc1c, g1c, sh2c, sc2c, g2c = jnp.split(cc_act @ ada_w[i] + ada_b[i], 6, axis=-1)
        lam_init = 0.8 - 0.6 * math.exp(-0.3 * i)
        lam = (jnp.exp(jnp.sum(lambda_q1[i].astype(jnp.float32) * lambda_k1[i].astype(jnp.float32)))
               - jnp.exp(jnp.sum(lambda_q2[i].astype(jnp.float32) * lambda_k2[i].astype(jnp.float32)))
               + lam_init)

        xn = _modulate(_rmsnorm(x, norm1_g[i]), sh1, sc1)
        cn = _modulate(_rmsnorm(ctx, norm1_g[i]), sh1c, sc1c)
        px = xn @ w_in[i]
        pc = cn @ w_in[i][:, s_q:N_IN_MIX]

        qx = _axial_rope(px[..., :s_q].reshape(b, l, ATT_HEADS, 2, ATT_DK), ang_r, ang_c)
        kx = _axial_rope(px[..., s_q:s_k].reshape(b, l, ATT_HEADS, 2, ATT_DK), ang_r, ang_c)
        vx = px[..., s_k:s_v].reshape(b, l, ATT_HEADS, ATT_DV)
        kc = pc[..., :N_K].reshape(b, n_ctx, ATT_HEADS, 2, ATT_DK)
        vc = pc[..., N_K:N_K + N_V].reshape(b, n_ctx, ATT_HEADS, ATT_DV)
        k_all = jnp.concatenate([kc, kx], axis=1)
        v_all = jnp.concatenate([vc, vx], axis=1)
        attn_x = _diff_post(_diff_attend_blocked(qx, k_all, v_all, lam), subln_g[i], lam_init)

        ux = px[..., s_v:N_IN_MIX].astype(jnp.float32).reshape(b, l, SSM_GROUPS, SSM_GROUP)
        uc = pc[..., N_K + N_V:].astype(jnp.float32).reshape(b, n_ctx, SSM_GROUPS, SSM_GROUP)
        disc_f = _s5_discretize(ssm_a_re[i, 0], ssm_a_im[i, 0], ssm_log_dt[i, 0], ssm_b_re[i, 0], ssm_b_im[i, 0])
        disc_b = _s5_discretize(ssm_a_re[i, 1], ssm_a_im[i, 1], ssm_log_dt[i, 1], ssm_b_re[i, 1], ssm_b_im[i, 1])
        hcf = _s5_states(uc, disc_f, None, False)
        hcb = _s5_states(uc, disc_b, None, True)
        hxf = _s5_states(ux, disc_f, (hcf[0][:, -1], hcf[1][:, -1]), False)
        hxb = _s5_states(ux, disc_b, (hcb[0][:, 0], hcb[1][:, 0]), True)
        ssm_x = _s5_output(ux, hxf, hxb, ssm_c_re[i], ssm_c_im[i], ssm_d[i], w_glu[i], b_glu[i], x.dtype)

        mix_x = _merge(attn_x, ssm_x, px[..., N_IN_MIX:], w_attn_up[i], w_ssm_up[i], w_out[i])

        if ctx_out:
            qc = (cn @ w_in[i][:, :s_q]).reshape(b, n_ctx, ATT_HEADS, 2, ATT_DK)
            attn_c = _diff_post(_diff_attend(qc, kc, vc, lam), subln_g[i], lam_init)
            ssm_cx = _s5_output(uc, hcf, hcb, ssm_c_re[i], ssm_c_im[i], ssm_d[i], w_glu[i], b_glu[i], ctx.dtype)
            mix_c = _merge(attn_c, ssm_cx, cn @ w_in[i][:, N_IN_MIX:], w_attn_up[i], w_ssm_up[i], w_out[i])

        x = x + g1 * mix_x

        xn2 = _modulate(_rmsnorm(x, norm2_g[i]), sh2, sc2)
        x = x + g2 * _peer(xn2, peer_w_query[i], peer_sub_k1[i], peer_sub_k2[i], peer_u[i], peer_v[i])

        if ctx_out:
            ctx = ctx + g1c * mix_c
            cn2 = _modulate(_rmsnorm(ctx, norm2_g[i]), sh2c, sc2c)
            ctx = ctx + g2c * _peer(cn2, peer_w_query[i], peer_sub_k1[i], peer_sub_k2[i], peer_u[i], peer_v[i])

    return _rmsnorm(x, final_norm_g)
```

```python
import functools
import math

import jax
import jax.numpy as jnp
from jax import lax
from jax.experimental import pallas as pl
from jax.experimental.pallas import tpu as pltpu

F32 = jnp.float32
BF16 = jnp.bfloat16
HIGHEST = lax.Precision.HIGHEST

LANES = 128
VMEM_LIMIT = 56 * 1024 * 1024

D_MODEL = 1024
NORM_EPS = 1e-6
GRID_W = 64
ATT_HEADS = 8
ATT_DK = 64
ATT_DV = 2 * ATT_DK
ROPE_BASE = 10000.0
SSM_GROUP = 16
SSM_W = D_MODEL // 2
SSM_GROUPS = SSM_W // SSM_GROUP
SSM_STATE = 64
SSM_CHUNK = 16
PEER_HEADS = 8
PEER_NKEYS = 128
PEER_EXPERTS = PEER_NKEYS * PEER_NKEYS
PEER_TOPK = 16
PEER_DKEY = 128
LAMBDA_INIT = 0.8 - 0.6 * math.exp(-0.3 * 0)

NEG_INF = float("-inf")


def _dot(a, b, precision=None):
    return jnp.dot(a, b, preferred_element_type=F32, precision=precision)


def _dot_nt(a, b):
    return lax.dot_general(a, b, (((1,), (1,)), ((), ())), preferred_element_type=F32)


def _sigmoid(z):
    return 1.0 / (1.0 + jnp.exp(-z))


def _gelu_tanh(z):
    return 0.5 * z * (1.0 + jnp.tanh(math.sqrt(2.0 / math.pi) * (z + 0.044715 * (z * z * z))))


def _norm_mod(xf, g, shift, scale):
    ms = jnp.mean(xf * xf, axis=-1, keepdims=True)
    xn = xf * lax.rsqrt(ms + NORM_EPS) * g
    return xn * (1.0 + scale) + shift


def _params(sem):
    return pltpu.CompilerParams(dimension_semantics=sem, vmem_limit_bytes=VMEM_LIMIT)


def _const_spec(shape):
    nd = len(shape)
    return pl.BlockSpec(shape, lambda *_: (0,) * nd, pipeline_mode=pl.Buffered(1))


def _adaln_kernel(c_ref, w_ref, b_ref, o_ref):
    a = c_ref[...]
    act = a * _sigmoid(a)
    o_ref[...] = _dot(act, w_ref[...], precision=HIGHEST) + b_ref[...]


def _adaln(cin, w, b):
    n = w.shape[1]
    tn = 1536
    return pl.pallas_call(
        _adaln_kernel,
        grid=(n // tn,),
        in_specs=[pl.BlockSpec((8, D_MODEL), lambda j: (0, 0)),
                  pl.BlockSpec((D_MODEL, tn), lambda j: (0, j)),
                  pl.BlockSpec((1, tn), lambda j: (0, j))],
        out_specs=pl.BlockSpec((8, tn), lambda j: (0, j)),
        out_shape=jax.ShapeDtypeStruct((8, n), F32),
        compiler_params=_params(("arbitrary",)),
        name="adaln",
    )(cin, w, b)


def _rope_store(h, cos, sin, low, scale, o_ref):
    for hh in range(ATT_HEADS):
        blk = h[:, hh * LANES:(hh + 1) * LANES]
        sw = jnp.where(low, pltpu.roll(blk, LANES - 16, 1), pltpu.roll(blk, 16, 1))
        o_ref[:, hh * LANES:(hh + 1) * LANES] = ((blk * cos + sw * sin) * scale).astype(o_ref.dtype)


def _inproj_x_kernel(x_ref, mod_ref, g_ref, cos_ref, sin_ref, wq_ref, wk_ref, wv_ref, wu_ref, wg_ref,
                     q_ref, k_ref, v_ref, u_ref, sg_ref):
    xb = _norm_mod(x_ref[...], g_ref[...], mod_ref[0:1, :], mod_ref[1:2, :]).astype(BF16)
    cos = cos_ref[...]
    sin = sin_ref[...]
    lane = lax.broadcasted_iota(jnp.int32, (1, LANES), 1)
    low = (lane & 31) < 16
    _rope_store(_dot(xb, wq_ref[...]), cos, sin, low, ATT_DK ** -0.5, q_ref)
    _rope_store(_dot(xb, wk_ref[...]), cos, sin, low, 1.0, k_ref)
    v_ref[...] = _dot(xb, wv_ref[...]).astype(v_ref.dtype)
    u_ref[...] = _dot(xb, wu_ref[...])
    sg_ref[...] = _sigmoid(_dot(xb, wg_ref[...])).astype(sg_ref.dtype)


def _inproj_x(x, mod, g, cos, sin, wq, wk, wv, wu, wg, tm):
    b, l, d = x.shape
    tok = lambda n: pl.BlockSpec((None, tm, n), lambda bi, i: (bi, i, 0))
    return pl.pallas_call(
        _inproj_x_kernel,
        grid=(b, l // tm),
        in_specs=[tok(d),
                  pl.BlockSpec((None, 2, d), lambda bi, i: (bi, 0, 0)),
                  pl.BlockSpec((1, d), lambda bi, i: (0, 0)),
                  pl.BlockSpec((tm, LANES), lambda bi, i: (i, 0)),
                  pl.BlockSpec((tm, LANES), lambda bi, i: (i, 0)),
                  _const_spec(wq.shape), _const_spec(wk.shape), _const_spec(wv.shape),
                  _const_spec(wu.shape), _const_spec(wg.shape)],
        out_specs=[tok(d), tok(d), tok(d), tok(SSM_W), tok(2 * d)],
        out_shape=[jax.ShapeDtypeStruct((b, l, d), BF16),
                   jax.ShapeDtypeStruct((b, l, d), BF16),
                   jax.ShapeDtypeStruct((b, l, d), BF16),
                   jax.ShapeDtypeStruct((b, l, SSM_W), F32),
                   jax.ShapeDtypeStruct((b, l, 2 * d), BF16)],
        compiler_params=_params(("arbitrary", "arbitrary")),
        name="inproj_x",
    )(x, mod, g, cos, sin, wq, wk, wv, wu, wg)


def _inproj_c_kernel(x_ref, mod_ref, g_ref, wk_ref, wv_ref, wu_ref, k_ref, v_ref, u_ref):
    xb = _norm_mod(x_ref[...], g_ref[...], mod_ref[0:1, :], mod_ref[1:2, :]).astype(BF16)
    k_ref[...] = _dot(xb, wk_ref[...]).astype(k_ref.dtype)
    v_ref[...] = _dot(xb, wv_ref[...]).astype(v_ref.dtype)
    u_ref[...] = _dot(xb, wu_ref[...])


def _inproj_c(ctx, mod, g, wk, wv, wu):
    b, n, d = ctx.shape
    tok = lambda w: pl.BlockSpec((None, n, w), lambda bi: (bi, 0, 0))
    return pl.pallas_call(
        _inproj_c_kernel,
        grid=(b,),
        in_specs=[tok(d),
                  pl.BlockSpec((2, d), lambda bi: (0, 0)),
                  pl.BlockSpec((1, d), lambda bi: (0, 0)),
                  _const_spec(wk.shape), _const_spec(wv.shape), _const_spec(wu.shape)],
        out_specs=[tok(d), tok(d), tok(SSM_W)],
        out_shape=[jax.ShapeDtypeStruct((b, n, d), BF16),
                   jax.ShapeDtypeStruct((b, n, d), BF16),
                   jax.ShapeDtypeStruct((b, n, SSM_W), F32)],
        compiler_params=_params(("arbitrary",)),
        name="inproj_ctx",
    )(ctx, mod, g, wk, wv, wu)


def _attn_kernel(q_ref, kc_ref, vc_ref, kx_ref, vx_ref, lq1_ref, lk1_ref, lq2_ref, lk2_ref, sg_ref, o_ref,
                 *, tq, tk):
    q = q_ref[...]
    lane = lax.broadcasted_iota(jnp.int32, (1, LANES), 1)
    zero = jnp.zeros_like(q)
    qs = jnp.concatenate([jnp.where(lane < ATT_DK, q, zero), jnp.where(lane >= ATT_DK, q, zero)], axis=0)

    def step(carry, kblk, vblk):
        m, l, acc = carry
        s = _dot_nt(qs, kblk)
        m_new = jnp.maximum(m, jnp.max(s, axis=1, keepdims=True))
        alpha = jnp.exp(m - m_new)
        p = jnp.exp(s - m_new)
        l = alpha * l + jnp.sum(p, axis=1, keepdims=True)
        acc = alpha * acc + _dot(p.astype(BF16), vblk)
        return m_new, l, acc

    carry = (jnp.full((2 * tq, 1), NEG_INF, F32), jnp.zeros((2 * tq, 1), F32), jnp.zeros((2 * tq, ATT_DV), F32))
    carry = step(carry, kc_ref[...], vc_ref[...])

    def body(j, carry):
        off = pl.multiple_of(j * tk, tk)
        return step(carry, kx_ref[pl.ds(off, tk), :], vx_ref[pl.ds(off, tk), :])

    m, l, acc = lax.fori_loop(0, kx_ref.shape[0] // tk, body, carry)
    o = acc / l
    lam = (jnp.exp(jnp.sum(lq1_ref[...] * lk1_ref[...], axis=1, keepdims=True))
           - jnp.exp(jnp.sum(lq2_ref[...] * lk2_ref[...], axis=1, keepdims=True)) + LAMBDA_INIT)
    o = o[:tq] - lam * o[tq:]
    ms = jnp.mean(o * o, axis=-1, keepdims=True)
    o = o * lax.rsqrt(ms + NORM_EPS) * sg_ref[...] * (1.0 - LAMBDA_INIT)
    o_ref[...] = o.astype(o_ref.dtype)


def _attention(q, kc, vc, kx, vx, lq1, lk1, lq2, lk2, subln_g, tq, tk):
    b, l, d = q.shape
    nc = kc.shape[1]
    vec = lambda n: pl.BlockSpec((1, n), lambda bi, h, i: (0, 0))
    return pl.pallas_call(
        functools.partial(_attn_kernel, tq=tq, tk=tk),
        grid=(b, ATT_HEADS, l // tq),
        in_specs=[pl.BlockSpec((None, tq, LANES), lambda bi, h, i: (bi, i, h)),
                  pl.BlockSpec((None, nc, LANES), lambda bi, h, i: (bi, 0, h)),
                  pl.BlockSpec((None, nc, LANES), lambda bi, h, i: (bi, 0, h)),
                  pl.BlockSpec((None, l, LANES), lambda bi, h, i: (bi, 0, h)),
                  pl.BlockSpec((None, l, LANES), lambda bi, h, i: (bi, 0, h)),
                  vec(ATT_DK), vec(ATT_DK), vec(ATT_DK), vec(ATT_DK), vec(ATT_DV)],
        out_specs=pl.BlockSpec((None, tq, LANES), lambda bi, h, i: (bi, i, h)),
        out_shape=jax.ShapeDtypeStruct((b, l, d), BF16),
        compiler_params=_params(("arbitrary", "arbitrary", "arbitrary")),
        name="diff_attn",
    )(q, kc, vc, kx, vx, lq1, lk1, lq2, lk2, subln_g)


def _ssm_weights(a_re, a_im, log_dt, b_re, b_im, c_re, c_im, d_skip, nsteps):
    ch, g, n, p = SSM_CHUNK, SSM_GROUPS, SSM_STATE, SSM_GROUP
    ar = a_re.astype(F32)
    ai = a_im.astype(F32)
    dt = jnp.exp(log_dt.astype(F32))[..., None]
    lam = ar * dt
    theta = ai * dt
    mag = jnp.exp(lam)
    abar_re = mag * jnp.cos(theta)
    abar_im = mag * jnp.sin(theta)
    den = ar * ar + ai * ai
    nr = abar_re - 1.0
    ni = abar_im
    fr = (nr * ar + ni * ai) / den
    fi = (ni * ar - nr * ai) / den
    br = b_re.astype(F32)
    bi = b_im.astype(F32)
    bb_re = fr[..., None] * br - fi[..., None] * bi
    bb_im = fr[..., None] * bi + fi[..., None] * br
    cr = c_re.astype(F32)
    ci = c_im.astype(F32)

    j = jnp.arange(ch + 1, dtype=F32)[:, None, None, None]
    pr = jnp.exp(lam[None] * j) * jnp.cos(theta[None] * j)
    pi = jnp.exp(lam[None] * j) * jnp.sin(theta[None] * j)

    cb_re = cr[..., None] * bb_re[:, :, None] - ci[..., None] * bb_im[:, :, None]
    cb_im = cr[..., None] * bb_im[:, :, None] + ci[..., None] * bb_re[:, :, None]
    kker = (jnp.einsum("dgpnq,ldgn->dlgpq", cb_re, pr[:ch], precision=HIGHEST)
            - jnp.einsum("dgpnq,ldgn->dlgpq", cb_im, pi[:ch], precision=HIGHEST))
    ii = jnp.arange(ch)[:, None]
    jj = jnp.arange(ch)[None, :]
    kf = jnp.where((jj >= ii)[..., None, None, None], kker[0][jnp.clip(jj - ii, 0, ch - 1)], 0.0)
    kb = jnp.where((ii >= jj)[..., None, None, None], kker[1][jnp.clip(ii - jj, 0, ch - 1)], 0.0)
    mm = jnp.transpose(kf + kb, (2, 0, 4, 1, 3)).reshape(g, ch * p, ch * p)
    dvec = jnp.tile(d_skip.astype(F32).reshape(g, 1, p), (1, ch, 1)).reshape(g, ch * p)
    mm = mm + jnp.eye(ch * p, dtype=F32)[None] * dvec[:, None, :]

    def bpow(d, idx):
        pre = pr[idx, d][:, :, :, None]
        pim = pi[idx, d][:, :, :, None]
        re = pre * bb_re[d][None] - pim * bb_im[d][None]
        im = pre * bb_im[d][None] + pim * bb_re[d][None]
        both = jnp.stack([re, im], axis=0)
        return jnp.transpose(both, (2, 1, 4, 0, 3)).reshape(g, ch * p, 2 * n)

    def cpow(d, idx):
        pre = pr[idx, d][:, :, None, :]
        pim = pi[idx, d][:, :, None, :]
        ca_re = cr[d][None] * pre - ci[d][None] * pim
        ca_im = cr[d][None] * pim + ci[d][None] * pre
        both = jnp.stack([ca_re, -ca_im], axis=0)
        return jnp.transpose(both, (2, 0, 4, 1, 3)).reshape(g, 2 * n, ch * p)

    ar_idx = jnp.arange(ch)
    bpf = bpow(0, ch - 1 - ar_idx)
    bpb = bpow(1, ar_idx)
    cpf = cpow(0, ar_idx + 1)
    cpb = cpow(1, ch - ar_idx)

    re, im = pr[ch], pi[ch]
    rows1, rows2 = [], []
    for _ in range(nsteps):
        rows1.append(jnp.concatenate([re, re], axis=-1))
        rows2.append(jnp.concatenate([-im, im], axis=-1))
        re, im = re * re - im * im, 2.0 * re * im
    pad = [jnp.zeros_like(rows1[0])] * (16 - nsteps)
    coef = jnp.stack([jnp.stack(rows1 + pad, axis=2), jnp.stack(rows2 + pad, axis=2)], axis=2)
    coef = jnp.transpose(coef, (1, 0, 2, 3, 4)).reshape(g, 4, 16, 2 * n)
    return mm, bpf, bpb, cpf, cpb, coef


def _ssm_kernel(ux_ref, uc_ref, m_ref, bpf_ref, bpb_ref, cpf_ref, cpb_ref, coef_ref, y_ref, *, nb, nsteps):
    ux = ux_ref[...]
    uc = uc_ref[...]
    rx, rc = ux.shape[0], uc.shape[0]
    r = rx + rc
    bpf = bpf_ref[...]
    bpb = bpb_ref[...]
    hf = jnp.concatenate([_dot(uc, bpf, HIGHEST), _dot(ux, bpf, HIGHEST)], axis=0)
    gb = jnp.concatenate([_dot(ux, bpb, HIGHEST), _dot(uc, bpb, HIGHEST)], axis=0)
    row = lax.broadcasted_iota(jnp.int32, (r, 1), 0)
    half = SSM_STATE
    for k in range(nsteps):
        s = nb << k
        if s >= r:
            break
        sh = jnp.where(row >= s, pltpu.roll(hf, s, 0), 0.0)
        hf = hf + coef_ref[0, k:k + 1, :] * sh + coef_ref[1, k:k + 1, :] * pltpu.roll(sh, half, 1)
        sh = jnp.where(row < r - s, pltpu.roll(gb, r - s, 0), 0.0)
        gb = gb + coef_ref[2, k:k + 1, :] * sh + coef_ref[3, k:k + 1, :] * pltpu.roll(sh, half, 1)
    hprev = jnp.where(row >= nb, pltpu.roll(hf, nb, 0), 0.0)[rc:]
    gnext = jnp.where(row < r - nb, pltpu.roll(gb, r - nb, 0), 0.0)[:rx]
    y_ref[...] = (_dot(ux, m_ref[...], HIGHEST) + _dot(hprev, cpf_ref[...], HIGHEST)
                  + _dot(gnext, cpb_ref[...], HIGHEST))


def _ssm(ux, uc, weights, nb, nsteps):
    g, rx, w = ux.shape
    rc = uc.shape[1]
    mm, bpf, bpb, cpf, cpb, coef = weights
    per_g = lambda a: pl.BlockSpec((None,) + a.shape[1:], lambda gi: (gi,) + (0,) * (a.ndim - 1))
    return pl.pallas_call(
        functools.partial(_ssm_kernel, nb=nb, nsteps=nsteps),
        grid=(g,),
        in_specs=[per_g(ux), per_g(uc), per_g(mm), per_g(bpf), per_g(bpb), per_g(cpf), per_g(cpb), per_g(coef)],
        out_specs=pl.BlockSpec((None, rx, w), lambda gi: (gi, 0, 0)),
        out_shape=jax.ShapeDtypeStruct((g, rx, w), F32),
        compiler_params=_params(("arbitrary",)),
        name="s5_chunked",
    )(ux, uc, mm, bpf, bpb, cpf, cpb, coef)


def _merge_kernel(x_ref, attn_ref, y_ref, sg_ref, mod_ref, g2_ref, wglu_ref, bglu_ref, wau_ref, wsu_ref, wout_ref,
                  wqry_ref, k1_ref, k2_ref, x1_ref, xn_ref, s1_ref, s2_ref):
    d = D_MODEL
    yg = _gelu_tanh(y_ref[...])
    ysg = yg * _sigmoid(_dot(yg.astype(BF16), wglu_ref[...]) + bglu_ref[...])
    a_up = _dot(attn_ref[...], wau_ref[...])
    s_up = _dot(ysg.astype(BF16), wsu_ref[...])
    mix_in = sg_ref[:, :d].astype(F32) * a_up + sg_ref[:, d:].astype(F32) * s_up
    mix = _dot(mix_in.astype(BF16), wout_ref[...])
    x1 = x_ref[...] + mod_ref[0:1, :] * mix
    x1_ref[...] = x1
    xn = _norm_mod(x1, g2_ref[...], mod_ref[1:2, :], mod_ref[2:3, :]).astype(BF16)
    xn_ref[...] = xn
    qb = _dot(xn, wqry_ref[...]).astype(BF16)
    for h in range(PEER_HEADS):
        qh = qb[:, h * LANES:(h + 1) * LANES]
        s1_ref[h * LANES:(h + 1) * LANES, :] = _dot_nt(k1_ref[...], qh)
        s2_ref[h * LANES:(h + 1) * LANES, :] = _dot_nt(k2_ref[...], qh)


def _merge(x, attn, y, sg, mod, g2, wglu, bglu, wau, wsu, wout, wqry, k1p, k2p, tm):
    b, l, d = x.shape
    ntok = b * l
    nt = l // tm
    tok = lambda n: pl.BlockSpec((None, tm, n), lambda bi, i: (bi, i, 0))
    tokT = pl.BlockSpec((PEER_HEADS * PEER_NKEYS, tm), lambda bi, i: (0, bi * nt + i))
    return pl.pallas_call(
        _merge_kernel,
        grid=(b, nt),
        in_specs=[tok(d), tok(d), tok(SSM_W), tok(2 * d),
                  pl.BlockSpec((None, 8, d), lambda bi, i: (bi, 0, 0)),
                  pl.BlockSpec((1, d), lambda bi, i: (0, 0)),
                  _const_spec(wglu.shape), _const_spec(bglu.shape), _const_spec(wau.shape),
                  _const_spec(wsu.shape), _const_spec(wout.shape), _const_spec(wqry.shape),
                  _const_spec(k1p.shape), _const_spec(k2p.shape)],
        out_specs=[tok(d), tok(d), tokT, tokT],
        out_shape=[jax.ShapeDtypeStruct((b, l, d), F32),
                   jax.ShapeDtypeStruct((b, l, d), BF16),
                   jax.ShapeDtypeStruct((PEER_HEADS * PEER_NKEYS, ntok), F32),
                   jax.ShapeDtypeStruct((PEER_HEADS * PEER_NKEYS, ntok), F32)],
        compiler_params=_params(("arbitrary", "arbitrary")),
        name="merge_peer_query",
    )(x, attn, y, sg, mod, g2, wglu, bglu, wau, wsu, wout, wqry, k1p, k2p)


def _top_values(s, with_rank):
    rows = lax.broadcasted_iota(jnp.int32, (PEER_TOPK, 1), 0)

    def body(k, carry):
        cur, vals, rank = carry
        m = jnp.max(cur, axis=0, keepdims=True)
        hit = cur >= m
        vals = jnp.where(rows == k, m, vals)
        if with_rank:
            rank = jnp.where(hit, k.astype(F32), rank)
        return jnp.where(hit, NEG_INF, cur), vals, rank

    init = (s, jnp.zeros((PEER_TOPK, s.shape[1]), F32),
            jnp.full(s.shape if with_rank else (1, 1), float(PEER_TOPK), F32))
    _, vals, rank = lax.fori_loop(0, PEER_TOPK, body, init)
    return vals, rank


def _route_kernel(s1_ref, s2_ref, nsel_ref, coef_ref, rank2_ref, e2_ref):
    def head(h, _):
        sl = pl.ds(pl.multiple_of(h * PEER_NKEYS, PEER_NKEYS), PEER_NKEYS)
        s1 = s1_ref[sl, :]
        s2 = s2_ref[sl, :]
        v1, _ = _top_values(s1, False)
        v2, rank2 = _top_values(s2, True)
        cand = jnp.concatenate([v1[a:a + 1, :] + v2 for a in range(PEER_TOPK)], axis=0)
        mx = v1[0:1, :] + v2[0:1, :]

        def body(k, carry):
            cur, m = carry
            m = jnp.max(cur, axis=0, keepdims=True)
            return jnp.where(cur >= m, NEG_INF, cur), m

        _, tau = lax.fori_loop(0, PEER_TOPK, body, (cand, mx))
        z = jnp.sum(jnp.where(cand >= tau, jnp.exp(cand - mx), 0.0), axis=0, keepdims=True)
        nsel = jnp.zeros_like(s1)
        for bpos in range(PEER_TOPK):
            nsel = nsel + jnp.where(s1 + v2[bpos:bpos + 1, :] >= tau, 1.0, 0.0)
        nsel_ref[sl, :] = nsel
        coef_ref[sl, :] = jnp.exp(s1 - v1[0:1, :]) / z
        rank2_ref[sl, :] = rank2
        e2_ref[sl, :] = jnp.exp(s2 - v2[0:1, :])
        return 0

    lax.fori_loop(0, PEER_HEADS, head, 0)


def _route(s1t, s2t, tt):
    rows, ntok = s1t.shape
    spec = pl.BlockSpec((rows, tt), lambda i: (0, i))
    shp = jax.ShapeDtypeStruct((rows, ntok), F32)
    return pl.pallas_call(
        _route_kernel,
        grid=(ntok // tt,),
        in_specs=[spec, spec],
        out_specs=[spec, spec, spec, spec],
        out_shape=[shp, shp, shp, shp],
        compiler_params=_params(("arbitrary",)),
        name="peer_route",
    )(s1t, s2t)


def _peer_kernel(xn_ref, u_ref, vt_ref, rank2_ref, e2_ref, nsel_ref, coef_ref, x1_ref, mod_ref, gf_ref, o_ref,
                 act_ref, gate_ref, acc_ref, *, te, tt):
    j = pl.program_id(2)

    @pl.when(j == 0)
    def _():
        acc_ref[...] = jnp.zeros_like(acc_ref)

    act_ref[...] = _dot_nt(u_ref[...], xn_ref[...])
    nsub = te // PEER_NKEYS
    for lb in range(tt // LANES):
        ls = slice(lb * LANES, (lb + 1) * LANES)
        i1s = [pl.ds(pl.multiple_of(h * PEER_NKEYS + j * nsub, 8), nsub) for h in range(PEER_HEADS)]
        ns = [nsel_ref[i1s[h], ls] for h in range(PEER_HEADS)]
        cf = [coef_ref[i1s[h], ls] for h in range(PEER_HEADS)]
        for r in range(nsub):
            rs = slice(r * PEER_NKEYS, (r + 1) * PEER_NKEYS)
            w = jnp.zeros((PEER_NKEYS, LANES), F32)
            for h in range(PEER_HEADS):
                hs = slice(h * PEER_NKEYS, (h + 1) * PEER_NKEYS)
                w = w + jnp.where(rank2_ref[hs, ls] < ns[h][r:r + 1, :], e2_ref[hs, ls], 0.0) * cf[h][r:r + 1, :]
            gate_ref[rs, ls] = (w * _gelu_tanh(act_ref[rs, ls])).astype(BF16)
    acc_ref[...] += _dot(vt_ref[...], gate_ref[...])

    @pl.when(j == pl.num_programs(2) - 1)
    def _():
        x2 = x1_ref[...] + mod_ref[0:1, :] * acc_ref[...].T
        ms = jnp.mean(x2 * x2, axis=-1, keepdims=True)
        o_ref[...] = x2 * lax.rsqrt(ms + NORM_EPS) * gf_ref[...]


def _peer(xn, u, vt, rank2, e2t, nsel, coef, x1, mod, gf, tt, te):
    b, l, d = x1.shape
    nt = l // tt
    ne = PEER_EXPERTS // te
    rows = PEER_HEADS * PEER_NKEYS
    tok = pl.BlockSpec((None, tt, d), lambda bi, i, j: (bi, i, 0))
    tokT = pl.BlockSpec((rows, tt), lambda bi, i, j: (0, bi * nt + i))
    return pl.pallas_call(
        functools.partial(_peer_kernel, te=te, tt=tt),
        grid=(b, nt, ne),
        in_specs=[tok,
                  pl.BlockSpec((te, d), lambda bi, i, j: (j, 0)),
                  pl.BlockSpec((d, te), lambda bi, i, j: (0, j)),
                  tokT, tokT, tokT, tokT, tok,
                  pl.BlockSpec((None, 8, d), lambda bi, i, j: (bi, 0, 0)),
                  pl.BlockSpec((1, d), lambda bi, i, j: (0, 0))],
        out_specs=tok,
        out_shape=jax.ShapeDtypeStruct((b, l, d), F32),
        scratch_shapes=[pltpu.VMEM((te, tt), F32), pltpu.VMEM((te, tt), BF16), pltpu.VMEM((d, tt), F32)],
        compiler_params=_params(("arbitrary", "arbitrary", "arbitrary")),
        name="peer_dense",
    )(xn, u, vt, rank2, e2t, nsel, coef, x1, mod, gf)


def _rope_tables(l):
    n_freq = ATT_DK // 4
    inv = ROPE_BASE ** (-2.0 * jnp.arange(n_freq, dtype=F32) / (ATT_DK // 2))
    t = jnp.arange(l, dtype=jnp.int32)
    ang_r = (t // GRID_W).astype(F32)[:, None] * inv[None, :]
    ang_c = (t % GRID_W).astype(F32)[:, None] * inv[None, :]
    cos = jnp.concatenate([jnp.cos(ang_r)] * 2 + [jnp.cos(ang_c)] * 2, axis=-1)
    sin = jnp.concatenate([-jnp.sin(ang_r), jnp.sin(ang_r), -jnp.sin(ang_c), jnp.sin(ang_c)], axis=-1)
    return jnp.tile(cos, (1, 2)), jnp.tile(sin, (1, 2))


def _pick(n, pref):
    for t in pref:
        if n % t == 0:
            return t
    return n


def kernel(x, c, ctx, c_ctx, ada_w, ada_b, norm1_g, norm2_g, w_in, lambda_q1, lambda_k1, lambda_q2, lambda_k2, subln_g, w_attn_up, ssm_a_re, ssm_a_im, ssm_log_dt, ssm_b_re, ssm_b_im, ssm_c_re, ssm_c_im, ssm_d, w_glu, b_glu, w_ssm_up, w_out, peer_w_query, peer_sub_k1, peer_sub_k2, peer_u, peer_v, final_norm_g):
    assert ada_w.shape[0] == 1, "single-layer block"
    b, l, d = x.shape
    nc = ctx.shape[1]
    assert d == D_MODEL and b + 1 <= 8
    ch = SSM_CHUNK

    cin = jnp.concatenate([c, c_ctx[None, :], jnp.zeros((7 - b, d), F32)], axis=0)
    mod = _adaln(cin, ada_w[0], ada_b[0][None, :])
    sh1, sc1, g1, sh2, sc2, g2 = [mod[:, k * d:(k + 1) * d] for k in range(6)]
    mod1 = jnp.stack([sh1[:b], sc1[:b]], axis=1)
    mod1c = jnp.stack([sh1[b], sc1[b]], axis=0)
    zpad = jnp.zeros((b, 5, d), F32)
    mod2 = jnp.concatenate([jnp.stack([g1[:b], sh2[:b], sc2[:b]], axis=1), zpad], axis=1)
    mod3 = jnp.concatenate([g2[:b, None, :], jnp.zeros((b, 7, d), F32)], axis=1)

    w = w_in[0]
    n_q = 2 * ATT_HEADS * ATT_DK
    o_k, o_v, o_u, o_g = n_q, 2 * n_q, 2 * n_q + ATT_HEADS * ATT_DV, 2 * n_q + ATT_HEADS * ATT_DV + SSM_W
    wq, wk, wv, wu, wg = [w[:, a:e].astype(BF16) for a, e in
                          ((0, o_k), (o_k, o_v), (o_v, o_u), (o_u, o_g), (o_g, w.shape[1]))]
    cos, sin = _rope_tables(l)
    g1n = norm1_g[0][None, :]

    tm = _pick(l, (512, 256, 128))
    q, k, v, u, sg = _inproj_x(x, mod1, g1n, cos, sin, wq, wk, wv, wu, wg, tm)
    kc, vc, uc = _inproj_c(ctx, mod1c, g1n, wk, wv, wu)

    row = lambda a: a[0][None, :].astype(F32)
    attn = _attention(q, kc, vc, k, v, row(lambda_q1), row(lambda_k1), row(lambda_q2), row(lambda_k2),
                      row(subln_g), _pick(l, (256, 128)), _pick(l, (512, 256, 128)))

    nsteps = max(1, math.ceil(math.log2((l + nc) // ch)))
    sw = _ssm_weights(ssm_a_re[0], ssm_a_im[0], ssm_log_dt[0], ssm_b_re[0], ssm_b_im[0],
                      ssm_c_re[0], ssm_c_im[0], ssm_d[0], nsteps)
    to_groups = lambda a, n: jnp.transpose(a.reshape(b, n // ch, ch, SSM_GROUPS, SSM_GROUP),
                                           (3, 1, 0, 2, 4)).reshape(SSM_GROUPS, (n // ch) * b, ch * SSM_GROUP)
    yg = _ssm(to_groups(u, l), to_groups(uc, nc), sw, b, nsteps)
    y = jnp.transpose(yg.reshape(SSM_GROUPS, l // ch, b, ch, SSM_GROUP), (2, 1, 3, 0, 4)).reshape(b, l, SSM_W)

    half = PEER_DKEY // 2
    zk = jnp.zeros((PEER_NKEYS, half), F32)
    k1p = jnp.concatenate([peer_sub_k1[0], zk], axis=1).astype(BF16)
    k2p = jnp.concatenate([zk, peer_sub_k2[0]], axis=1).astype(BF16)
    x1, xn2, s1t, s2t = _merge(x, attn, y, sg, mod2, norm2_g[0][None, :],
                               w_glu[0].astype(BF16), b_glu[0][None, :], w_attn_up[0].astype(BF16),
                               w_ssm_up[0].astype(BF16), w_out[0].astype(BF16), peer_w_query[0].astype(BF16),
                               k1p, k2p, tm)

    nsel, coef, rank2, e2t = _route(s1t, s2t, _pick(b * l, (256, 128)))
    tt = _pick(l, (512, 256, 128))
    out = _peer(xn2, peer_u[0].astype(BF16), peer_v[0].T.astype(BF16), rank2, e2t, nsel, coef, x1, mod3,
                final_norm_g[None, :], tt, 1024)
    return out
```

```python
import functools
import math

import jax
import jax.numpy as jnp
from jax import lax
from jax.experimental import pallas as pl
from jax.experimental.pallas import tpu as pltpu

F32 = jnp.float32
BF16 = jnp.bfloat16
HIGHEST = lax.Precision.HIGHEST

LANES = 128
VMEM_LIMIT = 56 * 1024 * 1024

D_MODEL = 1024
NORM_EPS = 1e-6
GRID_W = 64
ATT_HEADS = 8
ATT_DK = 64
ATT_DV = 2 * ATT_DK
ROPE_BASE = 10000.0
SSM_GROUP = 16
SSM_W = D_MODEL // 2
SSM_GROUPS = SSM_W // SSM_GROUP
SSM_STATE = 64
SSM_CHUNK = 16
PEER_HEADS = 8
PEER_NKEYS = 128
PEER_EXPERTS = PEER_NKEYS * PEER_NKEYS
PEER_TOPK = 16
PEER_DKEY = 128
LAMBDA_INIT = 0.8 - 0.6 * math.exp(-0.3 * 0)

NEG_INF = float("-inf")


def _dot(a, b, precision=None):
    return jnp.dot(a, b, preferred_element_type=F32, precision=precision)


def _dot_nt(a, b):
    return lax.dot_general(a, b, (((1,), (1,)), ((), ())), preferred_element_type=F32)


def _sigmoid(z):
    return 1.0 / (1.0 + jnp.exp(-z))


def _gelu_tanh(z):
    return 0.5 * z * (1.0 + jnp.tanh(math.sqrt(2.0 / math.pi) * (z + 0.044715 * (z * z * z))))


def _norm_mod(xf, g, shift, scale):
    ms = jnp.mean(xf * xf, axis=-1, keepdims=True)
    xn = xf * lax.rsqrt(ms + NORM_EPS) * g
    return xn * (1.0 + scale) + shift


def _params(sem):
    return pltpu.CompilerParams(dimension_semantics=sem, vmem_limit_bytes=VMEM_LIMIT)


def _const_spec(shape):
    nd = len(shape)
    return pl.BlockSpec(shape, lambda *_: (0,) * nd, pipeline_mode=pl.Buffered(1))


def _adaln_kernel(c_ref, w_ref, b_ref, o_ref):
    a = c_ref[...]
    act = a * _sigmoid(a)
    o_ref[...] = _dot(act, w_ref[...], precision=HIGHEST) + b_ref[...]


def _adaln(cin, w, b):
    n = w.shape[1]
    tn = 1536
    return pl.pallas_call(
        _adaln_kernel,
        grid=(n // tn,),
        in_specs=[pl.BlockSpec((8, D_MODEL), lambda j: (0, 0)),
                  pl.BlockSpec((D_MODEL, tn), lambda j: (0, j)),
                  pl.BlockSpec((1, tn), lambda j: (0, j))],
        out_specs=pl.BlockSpec((8, tn), lambda j: (0, j)),
        out_shape=jax.ShapeDtypeStruct((8, n), F32),
        compiler_params=_params(("arbitrary",)),
        name="adaln",
    )(cin, w, b)


def _rope_store(h, cos, sin, low, scale, o_ref):
    for hh in range(ATT_HEADS):
        blk = h[:, hh * LANES:(hh + 1) * LANES]
        sw = jnp.where(low, pltpu.roll(blk, LANES - 16, 1), pltpu.roll(blk, 16, 1))
        o_ref[:, hh * LANES:(hh + 1) * LANES] = ((blk * cos + sw * sin) * scale).astype(o_ref.dtype)


def _inproj_x_kernel(x_ref, mod_ref, g_ref, cos_ref, sin_ref, wq_ref, wk_ref, wv_ref, wu_ref, wg_ref,
                     q_ref, k_ref, v_ref, u_ref, sg_ref):
    xb = _norm_mod(x_ref[...], g_ref[...], mod_ref[0:1, :], mod_ref[1:2, :]).astype(BF16)
    cos = cos_ref[...]
    sin = sin_ref[...]
    lane = lax.broadcasted_iota(jnp.int32, (1, LANES), 1)
    low = (lane & 31) < 16
    _rope_store(_dot(xb, wq_ref[...]), cos, sin, low, ATT_DK ** -0.5, q_ref)
    _rope_store(_dot(xb, wk_ref[...]), cos, sin, low, 1.0, k_ref)
    v_ref[...] = _dot(xb, wv_ref[...]).astype(v_ref.dtype)
    u_ref[...] = _dot(xb, wu_ref[...])
    sg_ref[...] = _sigmoid(_dot(xb, wg_ref[...])).astype(sg_ref.dtype)


def _inproj_x(x, mod, g, cos, sin, wq, wk, wv, wu, wg, tm):
    b, l, d = x.shape
    tok = lambda n: pl.BlockSpec((None, tm, n), lambda bi, i: (bi, i, 0))
    return pl.pallas_call(
        _inproj_x_kernel,
        grid=(b, l // tm),
        in_specs=[tok(d),
                  pl.BlockSpec((None, 2, d), lambda bi, i: (bi, 0, 0)),
                  pl.BlockSpec((1, d), lambda bi, i: (0, 0)),
                  pl.BlockSpec((tm, LANES), lambda bi, i: (i, 0)),
                  pl.BlockSpec((tm, LANES), lambda bi, i: (i, 0)),
                  _const_spec(wq.shape), _const_spec(wk.shape), _const_spec(wv.shape),
                  _const_spec(wu.shape), _const_spec(wg.shape)],
        out_specs=[tok(d), tok(d), tok(d), tok(SSM_W), tok(2 * d)],
        out_shape=[jax.ShapeDtypeStruct((b, l, d), BF16),
                   jax.ShapeDtypeStruct((b, l, d), BF16),
                   jax.ShapeDtypeStruct((b, l, d), BF16),
                   jax.ShapeDtypeStruct((b, l, SSM_W), F32),
                   jax.ShapeDtypeStruct((b, l, 2 * d), BF16)],
        compiler_params=_params(("arbitrary", "arbitrary")),
        name="inproj_x",
    )(x, mod, g, cos, sin, wq, wk, wv, wu, wg)


def _inproj_c_kernel(x_ref, mod_ref, g_ref, wk_ref, wv_ref, wu_ref, k_ref, v_ref, u_ref):
    xb = _norm_mod(x_ref[...], g_ref[...], mod_ref[0:1, :], mod_ref[1:2, :]).astype(BF16)
    k_ref[...] = _dot(xb, wk_ref[...]).astype(k_ref.dtype)
    v_ref[...] = _dot(xb, wv_ref[...]).astype(v_ref.dtype)
    u_ref[...] = _dot(xb, wu_ref[...])


def _inproj_c(ctx, mod, g, wk, wv, wu):
    b, n, d = ctx.shape
    tok = lambda w: pl.BlockSpec((None, n, w), lambda bi: (bi, 0, 0))
    return pl.pallas_call(
        _inproj_c_kernel,
        grid=(b,),
        in_specs=[tok(d),
                  pl.BlockSpec((2, d), lambda bi: (0, 0)),
                  pl.BlockSpec((1, d), lambda bi: (0, 0)),
                  _const_spec(wk.shape), _const_spec(wv.shape), _const_spec(wu.shape)],
        out_specs=[tok(d), tok(d), tok(SSM_W)],
        out_shape=[jax.ShapeDtypeStruct((b, n, d), BF16),
                   jax.ShapeDtypeStruct((b, n, d), BF16),
                   jax.ShapeDtypeStruct((b, n, SSM_W), F32)],
        compiler_params=_params(("arbitrary",)),
        name="inproj_ctx",
    )(ctx, mod, g, wk, wv, wu)


def _attn_kernel(q_ref, kc_ref, vc_ref, kx_ref, vx_ref, lq1_ref, lk1_ref, lq2_ref, lk2_ref, sg_ref, o_ref,
                 *, tq, tk):
    q = q_ref[...]
    lane = lax.broadcasted_iota(jnp.int32, (1, LANES), 1)
    zero = jnp.zeros_like(q)
    qs = jnp.concatenate([jnp.where(lane < ATT_DK, q, zero), jnp.where(lane >= ATT_DK, q, zero)], axis=0)

    def step(carry, kblk, vblk):
        m, l, acc = carry
        s = _dot_nt(qs, kblk)
        m_new = jnp.maximum(m, jnp.max(s, axis=1, keepdims=True))
        alpha = jnp.exp(m - m_new)
        p = jnp.exp(s - m_new)
        l = alpha * l + jnp.sum(p, axis=1, keepdims=True)
        acc = alpha * acc + _dot(p.astype(BF16), vblk)
        return m_new, l, acc

    carry = (jnp.full((2 * tq, 1), NEG_INF, F32), jnp.zeros((2 * tq, 1), F32), jnp.zeros((2 * tq, ATT_DV), F32))
    carry = step(carry, kc_ref[...], vc_ref[...])

    def body(j, carry):
        off = pl.multiple_of(j * tk, tk)
        return step(carry, kx_ref[pl.ds(off, tk), :], vx_ref[pl.ds(off, tk), :])

    m, l, acc = lax.fori_loop(0, kx_ref.shape[0] // tk, body, carry, unroll=True)
    o = acc / l
    lam = (jnp.exp(jnp.sum(lq1_ref[...] * lk1_ref[...], axis=1, keepdims=True))
           - jnp.exp(jnp.sum(lq2_ref[...] * lk2_ref[...], axis=1, keepdims=True)) + LAMBDA_INIT)
    o = o[:tq] - lam * o[tq:]
    ms = jnp.mean(o * o, axis=-1, keepdims=True)
    o = o * lax.rsqrt(ms + NORM_EPS) * sg_ref[...] * (1.0 - LAMBDA_INIT)
    o_ref[...] = o.astype(o_ref.dtype)


def _attention(q, kc, vc, kx, vx, lq1, lk1, lq2, lk2, subln_g, tq, tk):
    b, l, d = q.shape
    nc = kc.shape[1]
    vec = lambda n: pl.BlockSpec((1, n), lambda bi, h, i: (0, 0))
    return pl.pallas_call(
        functools.partial(_attn_kernel, tq=tq, tk=tk),
        grid=(b, ATT_HEADS, l // tq),
        in_specs=[pl.BlockSpec((None, tq, LANES), lambda bi, h, i: (bi, i, h)),
                  pl.BlockSpec((None, nc, LANES), lambda bi, h, i: (bi, 0, h)),
                  pl.BlockSpec((None, nc, LANES), lambda bi, h, i: (bi, 0, h)),
                  pl.BlockSpec((None, l, LANES), lambda bi, h, i: (bi, 0, h)),
                  pl.BlockSpec((None, l, LANES), lambda bi, h, i: (bi, 0, h)),
                  vec(ATT_DK), vec(ATT_DK), vec(ATT_DK), vec(ATT_DK), vec(ATT_DV)],
        out_specs=pl.BlockSpec((None, tq, LANES), lambda bi, h, i: (bi, i, h)),
        out_shape=jax.ShapeDtypeStruct((b, l, d), BF16),
        compiler_params=_params(("arbitrary", "arbitrary", "arbitrary")),
        name="diff_attn",
    )(q, kc, vc, kx, vx, lq1, lk1, lq2, lk2, subln_g)


def _ssm_weights(a_re, a_im, log_dt, b_re, b_im, c_re, c_im, d_skip, nsteps):
    ch, g, n, p = SSM_CHUNK, SSM_GROUPS, SSM_STATE, SSM_GROUP
    ar = a_re.astype(F32)
    ai = a_im.astype(F32)
    dt = jnp.exp(log_dt.astype(F32))[..., None]
    lam = ar * dt
    theta = ai * dt
    mag = jnp.exp(lam)
    abar_re = mag * jnp.cos(theta)
    abar_im = mag * jnp.sin(theta)
    den = ar * ar + ai * ai
    nr = abar_re - 1.0
    ni = abar_im
    fr = (nr * ar + ni * ai) / den
    fi = (ni * ar - nr * ai) / den
    br = b_re.astype(F32)
    bi = b_im.astype(F32)
    bb_re = fr[..., None] * br - fi[..., None] * bi
    bb_im = fr[..., None] * bi + fi[..., None] * br
    cr = c_re.astype(F32)
    ci = c_im.astype(F32)

    j = jnp.arange(ch + 1, dtype=F32)[:, None, None, None]
    pr = jnp.exp(lam[None] * j) * jnp.cos(theta[None] * j)
    pi = jnp.exp(lam[None] * j) * jnp.sin(theta[None] * j)

    cb_re = cr[..., None] * bb_re[:, :, None] - ci[..., None] * bb_im[:, :, None]
    cb_im = cr[..., None] * bb_im[:, :, None] + ci[..., None] * bb_re[:, :, None]
    kker = (jnp.einsum("dgpnq,ldgn->dlgpq", cb_re, pr[:ch], precision=HIGHEST)
            - jnp.einsum("dgpnq,ldgn->dlgpq", cb_im, pi[:ch], precision=HIGHEST))
    ii = jnp.arange(ch)[:, None]
    jj = jnp.arange(ch)[None, :]
    kf = jnp.where((jj >= ii)[..., None, None, None], kker[0][jnp.clip(jj - ii, 0, ch - 1)], 0.0)
    kb = jnp.where((ii >= jj)[..., None, None, None], kker[1][jnp.clip(ii - jj, 0, ch - 1)], 0.0)
    mm = jnp.transpose(kf + kb, (2, 0, 4, 1, 3)).reshape(g, ch * p, ch * p)
    dvec = jnp.tile(d_skip.astype(F32).reshape(g, 1, p), (1, ch, 1)).reshape(g, ch * p)
    mm = mm + jnp.eye(ch * p, dtype=F32)[None] * dvec[:, None, :]

    def bpow(d, idx):
        pre = pr[idx, d][:, :, :, None]
        pim = pi[idx, d][:, :, :, None]
        re = pre * bb_re[d][None] - pim * bb_im[d][None]
        im = pre * bb_im[d][None] + pim * bb_re[d][None]
        both = jnp.stack([re, im], axis=0)
        return jnp.transpose(both, (2, 1, 4, 0, 3)).reshape(g, ch * p, 2 * n)

    def cpow(d, idx):
        pre = pr[idx, d][:, :, None, :]
        pim = pi[idx, d][:, :, None, :]
        ca_re = cr[d][None] * pre - ci[d][None] * pim
        ca_im = cr[d][None] * pim + ci[d][None] * pre
        both = jnp.stack([ca_re, -ca_im], axis=0)
        return jnp.transpose(both, (2, 0, 4, 1, 3)).reshape(g, 2 * n, ch * p)

    ar_idx = jnp.arange(ch)
    bpf = bpow(0, ch - 1 - ar_idx)
    bpb = bpow(1, ar_idx)
    cpf = cpow(0, ar_idx + 1)
    cpb = cpow(1, ch - ar_idx)

    re, im = pr[ch], pi[ch]
    rows1, rows2 = [], []
    for _ in range(nsteps):
        rows1.append(jnp.concatenate([re, re], axis=-1))
        rows2.append(jnp.concatenate([-im, im], axis=-1))
        re, im = re * re - im * im, 2.0 * re * im
    pad = [jnp.zeros_like(rows1[0])] * (16 - nsteps)
    coef = jnp.stack([jnp.stack(rows1 + pad, axis=2), jnp.stack(rows2 + pad, axis=2)], axis=2)
    coef = jnp.transpose(coef, (1, 0, 2, 3, 4)).reshape(g, 4, 16, 2 * n)
    return mm, bpf, bpb, cpf, cpb, coef


def _ssm_kernel(ux_ref, uc_ref, m_ref, bpf_ref, bpb_ref, cpf_ref, cpb_ref, coef_ref, y_ref, *, nb, nsteps):
    ux = ux_ref[...]
    uc = uc_ref[...]
    rx, rc = ux.shape[0], uc.shape[0]
    r = rx + rc
    bpf = bpf_ref[...]
    bpb = bpb_ref[...]
    hf = jnp.concatenate([_dot(uc, bpf, HIGHEST), _dot(ux, bpf, HIGHEST)], axis=0)
    gb = jnp.concatenate([_dot(ux, bpb, HIGHEST), _dot(uc, bpb, HIGHEST)], axis=0)
    row = lax.broadcasted_iota(jnp.int32, (r, 1), 0)
    half = SSM_STATE
    for k in range(nsteps):
        s = nb << k
        if s >= r:
            break
        sh = jnp.where(row >= s, pltpu.roll(hf, s, 0), 0.0)
        hf = hf + coef_ref[0, k:k + 1, :] * sh + coef_ref[1, k:k + 1, :] * pltpu.roll(sh, half, 1)
        sh = jnp.where(row < r - s, pltpu.roll(gb, r - s, 0), 0.0)
        gb = gb + coef_ref[2, k:k + 1, :] * sh + coef_ref[3, k:k + 1, :] * pltpu.roll(sh, half, 1)
    hprev = jnp.where(row >= nb, pltpu.roll(hf, nb, 0), 0.0)[rc:]
    gnext = jnp.where(row < r - nb, pltpu.roll(gb, r - nb, 0), 0.0)[:rx]
    y_ref[...] = (_dot(ux, m_ref[...], HIGHEST) + _dot(hprev, cpf_ref[...], HIGHEST)
                  + _dot(gnext, cpb_ref[...], HIGHEST))


def _ssm(ux, uc, weights, nb, nsteps):
    g, rx, w = ux.shape
    rc = uc.shape[1]
    mm, bpf, bpb, cpf, cpb, coef = weights
    per_g = lambda a: pl.BlockSpec((None,) + a.shape[1:], lambda gi: (gi,) + (0,) * (a.ndim - 1))
    return pl.pallas_call(
        functools.partial(_ssm_kernel, nb=nb, nsteps=nsteps),
        grid=(g,),
        in_specs=[per_g(ux), per_g(uc), per_g(mm), per_g(bpf), per_g(bpb), per_g(cpf), per_g(cpb), per_g(coef)],
        out_specs=pl.BlockSpec((None, rx, w), lambda gi: (gi, 0, 0)),
        out_shape=jax.ShapeDtypeStruct((g, rx, w), F32),
        compiler_params=_params(("arbitrary",)),
        name="s5_chunked",
    )(ux, uc, mm, bpf, bpb, cpf, cpb, coef)


def _merge_kernel(x_ref, attn_ref, y_ref, sg_ref, mod_ref, g2_ref, wglu_ref, bglu_ref, wau_ref, wsu_ref, wout_ref,
                  wqry_ref, k1_ref, k2_ref, x1_ref, xn_ref, s1_ref, s2_ref):
    d = D_MODEL
    yg = _gelu_tanh(y_ref[...])
    ysg = yg * _sigmoid(_dot(yg.astype(BF16), wglu_ref[...]) + bglu_ref[...])
    a_up = _dot(attn_ref[...], wau_ref[...])
    s_up = _dot(ysg.astype(BF16), wsu_ref[...])
    mix_in = sg_ref[:, :d].astype(F32) * a_up + sg_ref[:, d:].astype(F32) * s_up
    mix = _dot(mix_in.astype(BF16), wout_ref[...])
    x1 = x_ref[...] + mod_ref[0:1, :] * mix
    x1_ref[...] = x1
    xn = _norm_mod(x1, g2_ref[...], mod_ref[1:2, :], mod_ref[2:3, :]).astype(BF16)
    xn_ref[...] = xn
    qb = _dot(xn, wqry_ref[...]).astype(BF16)
    for h in range(PEER_HEADS):
        qh = qb[:, h * LANES:(h + 1) * LANES]
        s1_ref[h * LANES:(h + 1) * LANES, :] = _dot_nt(k1_ref[...], qh)
        s2_ref[h * LANES:(h + 1) * LANES, :] = _dot_nt(k2_ref[...], qh)


def _merge(x, attn, y, sg, mod, g2, wglu, bglu, wau, wsu, wout, wqry, k1p, k2p, tm):
    b, l, d = x.shape
    ntok = b * l
    nt = l // tm
    tok = lambda n: pl.BlockSpec((None, tm, n), lambda bi, i: (bi, i, 0))
    tokT = pl.BlockSpec((PEER_HEADS * PEER_NKEYS, tm), lambda bi, i: (0, bi * nt + i))
    return pl.pallas_call(
        _merge_kernel,
        grid=(b, nt),
        in_specs=[tok(d), tok(d), tok(SSM_W), tok(2 * d),
                  pl.BlockSpec((None, 8, d), lambda bi, i: (bi, 0, 0)),
                  pl.BlockSpec((1, d), lambda bi, i: (0, 0)),
                  _const_spec(wglu.shape), _const_spec(bglu.shape), _const_spec(wau.shape),
                  _const_spec(wsu.shape), _const_spec(wout.shape), _const_spec(wqry.shape),
                  _const_spec(k1p.shape), _const_spec(k2p.shape)],
        out_specs=[tok(d), tok(d), tokT, tokT],
        out_shape=[jax.ShapeDtypeStruct((b, l, d), F32),
                   jax.ShapeDtypeStruct((b, l, d), BF16),
                   jax.ShapeDtypeStruct((PEER_HEADS * PEER_NKEYS, ntok), F32),
                   jax.ShapeDtypeStruct((PEER_HEADS * PEER_NKEYS, ntok), F32)],
        compiler_params=_params(("arbitrary", "arbitrary")),
        name="merge_peer_query",
    )(x, attn, y, sg, mod, g2, wglu, bglu, wau, wsu, wout, wqry, k1p, k2p)


CAND_ROWS = 80


def _pair_candidates(v1, v2):
    rows8 = lax.broadcasted_iota(jnp.int32, (8, 1), 0)
    blocks = [v1[0:1, :] + v2]
    for a in range(1, 8):
        blocks.append(jnp.where(rows8 < PEER_TOPK // (a + 1), v1[a:a + 1, :] + v2[0:8, :], NEG_INF))
    blocks.append(v1[8:16, :] + v2[0:1, :])
    return jnp.concatenate(blocks, axis=0)


def _route_kernel(s1_ref, s2_ref, nsel_ref, coef_ref, rank2_ref, e2_ref,
                  c1_ref, c2_ref, rk_ref, v1_ref, v2_ref, cand_ref, tau_ref):
    nh, nk, topk = PEER_HEADS, PEER_NKEYS, PEER_TOPK
    keys = lambda h: slice(h * nk, (h + 1) * nk)
    tops = lambda h: slice(h * topk, (h + 1) * topk)
    cands = lambda h: slice(h * CAND_ROWS, (h + 1) * CAND_ROWS)
    c1_ref[...] = s1_ref[...]
    c2_ref[...] = s2_ref[...]
    rk_ref[...] = jnp.full(rk_ref.shape, float(topk), F32)
    v1_ref[...] = jnp.zeros_like(v1_ref)
    v2_ref[...] = jnp.zeros_like(v2_ref)
    rows16 = lax.broadcasted_iota(jnp.int32, (topk, 1), 0)

    def extract(k, _):
        for h in range(nh):
            cur = c1_ref[keys(h), :]
            m = jnp.max(cur, axis=0, keepdims=True)
            c1_ref[keys(h), :] = jnp.where(cur >= m, NEG_INF, cur)
            v1_ref[tops(h), :] = jnp.where(rows16 == k, m, v1_ref[tops(h), :])
            cur = c2_ref[keys(h), :]
            m = jnp.max(cur, axis=0, keepdims=True)
            hit = cur >= m
            c2_ref[keys(h), :] = jnp.where(hit, NEG_INF, cur)
            rk_ref[keys(h), :] = jnp.where(hit, k.astype(F32), rk_ref[keys(h), :])
            v2_ref[tops(h), :] = jnp.where(rows16 == k, m, v2_ref[tops(h), :])
        return 0

    lax.fori_loop(0, topk, extract, 0)

    for h in range(nh):
        cand_ref[cands(h), :] = _pair_candidates(v1_ref[tops(h), :], v2_ref[tops(h), :])

    def knock(k, _):
        for h in range(nh):
            cur = cand_ref[cands(h), :]
            m = jnp.max(cur, axis=0, keepdims=True)
            cand_ref[cands(h), :] = jnp.where(cur >= m, NEG_INF, cur)
            tau_ref[h:h + 1, :] = m
        return 0

    lax.fori_loop(0, topk, knock, 0)

    for h in range(nh):
        v1 = v1_ref[tops(h), :]
        v2 = v2_ref[tops(h), :]
        tau = tau_ref[h:h + 1, :]
        cand = _pair_candidates(v1, v2)
        mx = v1[0:1, :] + v2[0:1, :]
        z = jnp.sum(jnp.where(cand >= tau, jnp.exp(cand - mx), 0.0), axis=0, keepdims=True)
        s1 = s1_ref[keys(h), :]
        nsel = jnp.zeros_like(s1)
        for bpos in range(topk):
            nsel = nsel + jnp.where(s1 + v2[bpos:bpos + 1, :] >= tau, 1.0, 0.0)
        nsel_ref[keys(h), :] = jnp.where(s1 >= v1[topk - 1:topk, :], nsel, 0.0)
        coef_ref[keys(h), :] = jnp.exp(s1 - v1[0:1, :]) / z
        rank2_ref[keys(h), :] = rk_ref[keys(h), :].astype(rank2_ref.dtype)
        e2_ref[keys(h), :] = jnp.exp(s2_ref[keys(h), :] - v2[0:1, :]).astype(e2_ref.dtype)


def _route(s1t, s2t, tt):
    rows, ntok = s1t.shape
    spec = pl.BlockSpec((rows, tt), lambda i: (0, i))
    f32 = jax.ShapeDtypeStruct((rows, ntok), F32)
    big = pltpu.VMEM((rows, tt), F32)
    small = pltpu.VMEM((PEER_HEADS * PEER_TOPK, tt), F32)
    return pl.pallas_call(
        _route_kernel,
        grid=(ntok // tt,),
        in_specs=[spec, spec],
        out_specs=[spec, spec, spec, spec],
        out_shape=[f32, f32, f32, f32],
        scratch_shapes=[big, big, big, small, small,
                        pltpu.VMEM((PEER_HEADS * CAND_ROWS, tt), F32), pltpu.VMEM((PEER_HEADS, tt), F32)],
        compiler_params=_params(("arbitrary",)),
        name="peer_route",
    )(s1t, s2t)


def _bf16_pair_words(x):
    hi = pltpu.bitcast(x.astype(BF16).astype(F32), jnp.uint32)
    return hi | (hi >> 16)


def _bf16_rows(word_row, n):
    return pltpu.bitcast(jnp.broadcast_to(word_row, (n // 2, LANES)), BF16)


def _peer_step(first_key, xn_ref, u_ref, vt_ref, rank2_ref, e2_ref, nsel_ref, coef_ref,
               act_next_ref, act_ref, gate_ref, acc_ref, *, te, tt):
    act_next_ref[...] = _dot_nt(u_ref[...], xn_ref[...])
    nsub = te // PEER_NKEYS
    zero = jnp.zeros((PEER_NKEYS, LANES), BF16)
    for lb in range(tt // LANES):
        ls = slice(lb * LANES, (lb + 1) * LANES)
        i1s = [pl.ds(pl.multiple_of(h * PEER_NKEYS + first_key, 8), nsub) for h in range(PEER_HEADS)]
        ns = [_bf16_pair_words(nsel_ref[i1s[h], ls]) for h in range(PEER_HEADS)]
        cf = [_bf16_pair_words(coef_ref[i1s[h], ls]) for h in range(PEER_HEADS)]
        for r in range(nsub):
            rs = slice(r * PEER_NKEYS, (r + 1) * PEER_NKEYS)
            w = zero
            for h in range(PEER_HEADS):
                hs = slice(h * PEER_NKEYS, (h + 1) * PEER_NKEYS)
                sel = rank2_ref[hs, ls] < _bf16_rows(ns[h][r:r + 1, :], PEER_NKEYS)
                w = w + jnp.where(sel, e2_ref[hs, ls], zero) * _bf16_rows(cf[h][r:r + 1, :], PEER_NKEYS)
            gate_ref[rs, ls] = (w.astype(F32) * _gelu_tanh(act_ref[rs, ls])).astype(BF16)
    acc_ref[...] += _dot(vt_ref[...], gate_ref[...])


def _peer_kernel(xn_ref, u_ref, vt_ref, rank2_ref, e2_ref, nsel_ref, coef_ref, x1_ref, mod_ref, gf_ref, o_ref,
                 act0_ref, act1_ref, gate_ref, acc_ref, rank2b_ref, e2b_ref, *, te, tt):
    j = pl.program_id(2)
    last = pl.num_programs(2) - 1
    step = functools.partial(_peer_step, (j - 1) * (te // PEER_NKEYS), xn_ref, u_ref, vt_ref, rank2b_ref, e2b_ref,
                             nsel_ref, coef_ref, gate_ref=gate_ref, acc_ref=acc_ref, te=te, tt=tt)

    @pl.when(j == 0)
    def _():
        act0_ref[...] = _dot_nt(u_ref[...], xn_ref[...])
        acc_ref[...] = jnp.zeros_like(acc_ref)
        rank2b_ref[...] = rank2_ref[...].astype(BF16)
        e2b_ref[...] = e2_ref[...].astype(BF16)

    @pl.when(j % 2 == 1)
    def _():
        step(act_next_ref=act1_ref, act_ref=act0_ref)

    @pl.when(jnp.logical_and(j > 0, j % 2 == 0))
    def _():
        step(act_next_ref=act0_ref, act_ref=act1_ref)

    @pl.when(j == last)
    def _():
        x2 = x1_ref[...] + mod_ref[0:1, :] * acc_ref[...].T
        ms = jnp.mean(x2 * x2, axis=-1, keepdims=True)
        o_ref[...] = x2 * lax.rsqrt(ms + NORM_EPS) * gf_ref[...]


def _peer(xn, u, vt, rank2, e2t, nsel, coef, x1, mod, gf, tt, te):
    b, l, d = x1.shape
    nt = l // tt
    ne = PEER_EXPERTS // te
    rows = PEER_HEADS * PEER_NKEYS
    tok = pl.BlockSpec((None, tt, d), lambda bi, i, j: (bi, i, 0))
    tokT = pl.BlockSpec((rows, tt), lambda bi, i, j: (0, bi * nt + i))
    return pl.pallas_call(
        functools.partial(_peer_kernel, te=te, tt=tt),
        grid=(b, nt, ne + 1),
        in_specs=[tok,
                  pl.BlockSpec((te, d), lambda bi, i, j: (jnp.minimum(j, ne - 1), 0)),
                  pl.BlockSpec((d, te), lambda bi, i, j: (0, jnp.maximum(j - 1, 0))),
                  tokT, tokT, tokT, tokT, tok,
                  pl.BlockSpec((None, 8, d), lambda bi, i, j: (bi, 0, 0)),
                  pl.BlockSpec((1, d), lambda bi, i, j: (0, 0))],
        out_specs=tok,
        out_shape=jax.ShapeDtypeStruct((b, l, d), F32),
        scratch_shapes=[pltpu.VMEM((te, tt), F32), pltpu.VMEM((te, tt), F32), pltpu.VMEM((te, tt), BF16),
                        pltpu.VMEM((d, tt), F32), pltpu.VMEM((rows, tt), BF16), pltpu.VMEM((rows, tt), BF16)],
        compiler_params=_params(("arbitrary", "arbitrary", "arbitrary")),
        name="peer_dense",
    )(xn, u, vt, rank2, e2t, nsel, coef, x1, mod, gf)


def _rope_tables(l):
    n_freq = ATT_DK // 4
    inv = ROPE_BASE ** (-2.0 * jnp.arange(n_freq, dtype=F32) / (ATT_DK // 2))
    t = jnp.arange(l, dtype=jnp.int32)
    ang_r = (t // GRID_W).astype(F32)[:, None] * inv[None, :]
    ang_c = (t % GRID_W).astype(F32)[:, None] * inv[None, :]
    cos = jnp.concatenate([jnp.cos(ang_r)] * 2 + [jnp.cos(ang_c)] * 2, axis=-1)
    sin = jnp.concatenate([-jnp.sin(ang_r), jnp.sin(ang_r), -jnp.sin(ang_c), jnp.sin(ang_c)], axis=-1)
    return jnp.tile(cos, (1, 2)), jnp.tile(sin, (1, 2))


def _pick(n, pref):
    for t in pref:
        if n % t == 0:
            return t
    return n


def kernel(x, c, ctx, c_ctx, ada_w, ada_b, norm1_g, norm2_g, w_in, lambda_q1, lambda_k1, lambda_q2, lambda_k2, subln_g, w_attn_up, ssm_a_re, ssm_a_im, ssm_log_dt, ssm_b_re, ssm_b_im, ssm_c_re, ssm_c_im, ssm_d, w_glu, b_glu, w_ssm_up, w_out, peer_w_query, peer_sub_k1, peer_sub_k2, peer_u, peer_v, final_norm_g):
    assert ada_w.shape[0] == 1, "single-layer block"
    b, l, d = x.shape
    nc = ctx.shape[1]
    assert d == D_MODEL and b + 1 <= 8
    ch = SSM_CHUNK

    cin = jnp.concatenate([c, c_ctx[None, :], jnp.zeros((7 - b, d), F32)], axis=0)
    mod = _adaln(cin, ada_w[0], ada_b[0][None, :])
    sh1, sc1, g1, sh2, sc2, g2 = [mod[:, k * d:(k + 1) * d] for k in range(6)]
    mod1 = jnp.stack([sh1[:b], sc1[:b]], axis=1)
    mod1c = jnp.stack([sh1[b], sc1[b]], axis=0)
    zpad = jnp.zeros((b, 5, d), F32)
    mod2 = jnp.concatenate([jnp.stack([g1[:b], sh2[:b], sc2[:b]], axis=1), zpad], axis=1)
    mod3 = jnp.concatenate([g2[:b, None, :], jnp.zeros((b, 7, d), F32)], axis=1)

    w = w_in[0]
    n_q = 2 * ATT_HEADS * ATT_DK
    o_k, o_v, o_u, o_g = n_q, 2 * n_q, 2 * n_q + ATT_HEADS * ATT_DV, 2 * n_q + ATT_HEADS * ATT_DV + SSM_W
    wq, wk, wv, wu, wg = [w[:, a:e].astype(BF16) for a, e in
                          ((0, o_k), (o_k, o_v), (o_v, o_u), (o_u, o_g), (o_g, w.shape[1]))]
    cos, sin = _rope_tables(l)
    g1n = norm1_g[0][None, :]

    tm = _pick(l, (512, 256, 128))
    q, k, v, u, sg = _inproj_x(x, mod1, g1n, cos, sin, wq, wk, wv, wu, wg, tm)
    kc, vc, uc = _inproj_c(ctx, mod1c, g1n, wk, wv, wu)

    row = lambda a: a[0][None, :].astype(F32)
    attn = _attention(q, kc, vc, k, v, row(lambda_q1), row(lambda_k1), row(lambda_q2), row(lambda_k2),
                      row(subln_g), _pick(l, (256, 128)), _pick(l, (512, 256, 128)))

    nsteps = max(1, math.ceil(math.log2((l + nc) // ch)))
    sw = _ssm_weights(ssm_a_re[0], ssm_a_im[0], ssm_log_dt[0], ssm_b_re[0], ssm_b_im[0],
                      ssm_c_re[0], ssm_c_im[0], ssm_d[0], nsteps)
    to_groups = lambda a, n: jnp.transpose(a.reshape(b, n // ch, ch, SSM_GROUPS, SSM_GROUP),
                                           (3, 1, 0, 2, 4)).reshape(SSM_GROUPS, (n // ch) * b, ch * SSM_GROUP)
    yg = _ssm(to_groups(u, l), to_groups(uc, nc), sw, b, nsteps)
    y = jnp.transpose(yg.reshape(SSM_GROUPS, l // ch, b, ch, SSM_GROUP), (2, 1, 3, 0, 4)).reshape(b, l, SSM_W)

    half = PEER_DKEY // 2
    zk = jnp.zeros((PEER_NKEYS, half), F32)
    k1p = jnp.concatenate([peer_sub_k1[0], zk], axis=1).astype(BF16)
    k2p = jnp.concatenate([zk, peer_sub_k2[0]], axis=1).astype(BF16)
    x1, xn2, s1t, s2t = _merge(x, attn, y, sg, mod2, norm2_g[0][None, :],
                               w_glu[0].astype(BF16), b_glu[0][None, :], w_attn_up[0].astype(BF16),
                               w_ssm_up[0].astype(BF16), w_out[0].astype(BF16), peer_w_query[0].astype(BF16),
                               k1p, k2p, tm)

    nsel, coef, rank2, e2t = _route(s1t, s2t, _pick(b * l, (256, 128)))
    tt = _pick(l, (512, 256, 128))
    out = _peer(xn2, peer_u[0].astype(BF16), peer_v[0].T.astype(BF16), rank2, e2t, nsel, coef, x1, mod3,
                final_norm_g[None, :], tt, 1024)
    return out
```

```python
import functools
import math

import jax
import jax.numpy as jnp
from jax import lax
from jax.experimental import pallas as pl
from jax.experimental.pallas import tpu as pltpu

F32 = jnp.float32
BF16 = jnp.bfloat16
HIGHEST = lax.Precision.HIGHEST

LANES = 128
MXU_WIDTH = 256
VMEM_LIMIT = 56 * 1024 * 1024

D_MODEL = 1024
NORM_EPS = 1e-6
GRID_W = 64
ATT_HEADS = 8
ATT_DK = 64
ATT_DV = 2 * ATT_DK
ROPE_BASE = 10000.0
SSM_GROUP = 16
SSM_W = D_MODEL // 2
SSM_GROUPS = SSM_W // SSM_GROUP
SSM_STATE = 64
SSM_CHUNK = 16
PEER_HEADS = 8
PEER_NKEYS = 128
PEER_EXPERTS = PEER_NKEYS * PEER_NKEYS
PEER_TOPK = 16
PEER_DKEY = 128
LAMBDA_INIT = 0.8 - 0.6 * math.exp(-0.3 * 0)

NEG_INF = float("-inf")


def _dot(a, b, precision=None):
    return jnp.dot(a, b, preferred_element_type=F32, precision=precision)


def _dot_nt(a, b):
    return lax.dot_general(a, b, (((1,), (1,)), ((), ())), preferred_element_type=F32)


def _sigmoid(z):
    return 1.0 / (1.0 + jnp.exp(-z))


def _gelu_tanh(z):
    return 0.5 * z * (1.0 + jnp.tanh(math.sqrt(2.0 / math.pi) * (z + 0.044715 * (z * z * z))))


def _norm_mod(xf, g, shift, scale):
    ms = jnp.mean(xf * xf, axis=-1, keepdims=True)
    xn = xf * lax.rsqrt(ms + NORM_EPS) * g
    return xn * (1.0 + scale) + shift


def _params(sem):
    return pltpu.CompilerParams(dimension_semantics=sem, vmem_limit_bytes=VMEM_LIMIT)


def _const_spec(shape):
    nd = len(shape)
    return pl.BlockSpec(shape, lambda *_: (0,) * nd, pipeline_mode=pl.Buffered(1))


def _adaln_kernel(c_ref, w_ref, b_ref, o_ref):
    a = c_ref[...]
    act = a * _sigmoid(a)
    o_ref[...] = _dot(act, w_ref[...], precision=HIGHEST) + b_ref[...]


def _adaln(cin, w, b):
    n = w.shape[1]
    tn = 1536
    return pl.pallas_call(
        _adaln_kernel,
        grid=(n // tn,),
        in_specs=[pl.BlockSpec((8, D_MODEL), lambda j: (0, 0)),
                  pl.BlockSpec((D_MODEL, tn), lambda j: (0, j)),
                  pl.BlockSpec((1, tn), lambda j: (0, j))],
        out_specs=pl.BlockSpec((8, tn), lambda j: (0, j)),
        out_shape=jax.ShapeDtypeStruct((8, n), F32),
        compiler_params=_params(("arbitrary",)),
        name="adaln",
    )(cin, w, b)


def _rope_store(h, cos, sin, low, scale, o_ref):
    for hh in range(ATT_HEADS):
        blk = h[:, hh * LANES:(hh + 1) * LANES]
        sw = jnp.where(low, pltpu.roll(blk, LANES - 16, 1), pltpu.roll(blk, 16, 1))
        o_ref[:, hh * LANES:(hh + 1) * LANES] = ((blk * cos + sw * sin) * scale).astype(o_ref.dtype)


def _inproj_x_kernel(x_ref, mod_ref, g_ref, cos_ref, sin_ref, wq_ref, wk_ref, wv_ref, wu_ref, wg_ref,
                     q_ref, k_ref, v_ref, u_ref, sg_ref):
    xb = _norm_mod(x_ref[...], g_ref[...], mod_ref[0:1, :], mod_ref[1:2, :]).astype(BF16)
    cos = cos_ref[...]
    sin = sin_ref[...]
    lane = lax.broadcasted_iota(jnp.int32, (1, LANES), 1)
    low = (lane & 31) < 16
    _rope_store(_dot(xb, wq_ref[...]), cos, sin, low, ATT_DK ** -0.5, q_ref)
    _rope_store(_dot(xb, wk_ref[...]), cos, sin, low, 1.0, k_ref)
    v_ref[...] = _dot(xb, wv_ref[...]).astype(v_ref.dtype)
    u_ref[...] = _dot(xb, wu_ref[...])
    sg_ref[...] = _sigmoid(_dot(xb, wg_ref[...])).astype(sg_ref.dtype)


def _inproj_x(x, mod, g, cos, sin, wq, wk, wv, wu, wg, tm):
    b, l, d = x.shape
    tok = lambda n: pl.BlockSpec((None, tm, n), lambda bi, i: (bi, i, 0))
    return pl.pallas_call(
        _inproj_x_kernel,
        grid=(b, l // tm),
        in_specs=[tok(d),
                  pl.BlockSpec((None, 2, d), lambda bi, i: (bi, 0, 0)),
                  pl.BlockSpec((1, d), lambda bi, i: (0, 0)),
                  pl.BlockSpec((tm, LANES), lambda bi, i: (i, 0)),
                  pl.BlockSpec((tm, LANES), lambda bi, i: (i, 0)),
                  _const_spec(wq.shape), _const_spec(wk.shape), _const_spec(wv.shape),
                  _const_spec(wu.shape), _const_spec(wg.shape)],
        out_specs=[tok(d), tok(d), tok(d), tok(SSM_W), tok(2 * d)],
        out_shape=[jax.ShapeDtypeStruct((b, l, d), BF16),
                   jax.ShapeDtypeStruct((b, l, d), BF16),
                   jax.ShapeDtypeStruct((b, l, d), BF16),
                   jax.ShapeDtypeStruct((b, l, SSM_W), F32),
                   jax.ShapeDtypeStruct((b, l, 2 * d), BF16)],
        compiler_params=_params(("arbitrary", "arbitrary")),
        name="inproj_x",
    )(x, mod, g, cos, sin, wq, wk, wv, wu, wg)


def _inproj_c_kernel(x_ref, mod_ref, g_ref, wk_ref, wv_ref, wu_ref, k_ref, v_ref, u_ref):
    xb = _norm_mod(x_ref[...], g_ref[...], mod_ref[0:1, :], mod_ref[1:2, :]).astype(BF16)
    k_ref[...] = _dot(xb, wk_ref[...]).astype(k_ref.dtype)
    v_ref[...] = _dot(xb, wv_ref[...]).astype(v_ref.dtype)
    u_ref[...] = _dot(xb, wu_ref[...])


def _inproj_c(ctx, mod, g, wk, wv, wu):
    b, n, d = ctx.shape
    tok = lambda w: pl.BlockSpec((None, n, w), lambda bi: (bi, 0, 0))
    return pl.pallas_call(
        _inproj_c_kernel,
        grid=(b,),
        in_specs=[tok(d),
                  pl.BlockSpec((2, d), lambda bi: (0, 0)),
                  pl.BlockSpec((1, d), lambda bi: (0, 0)),
                  _const_spec(wk.shape), _const_spec(wv.shape), _const_spec(wu.shape)],
        out_specs=[tok(d), tok(d), tok(SSM_W)],
        out_shape=[jax.ShapeDtypeStruct((b, n, d), BF16),
                   jax.ShapeDtypeStruct((b, n, d), BF16),
                   jax.ShapeDtypeStruct((b, n, SSM_W), F32)],
        compiler_params=_params(("arbitrary",)),
        name="inproj_ctx",
    )(ctx, mod, g, wk, wv, wu)


def _attn_kernel(q_ref, kc_ref, vc_ref, kx_ref, vx_ref, lq1_ref, lk1_ref, lq2_ref, lk2_ref, sg_ref, o_ref,
                 *, tq, tk):
    q = q_ref[...]
    lane = lax.broadcasted_iota(jnp.int32, (1, LANES), 1)
    zero = jnp.zeros_like(q)
    qs = jnp.concatenate([jnp.where(lane < ATT_DK, q, zero), jnp.where(lane >= ATT_DK, q, zero)], axis=0)

    def step(carry, kblk, vblk):
        m, l, acc = carry
        s = _dot_nt(qs, kblk)
        m_new = jnp.maximum(m, jnp.max(s, axis=1, keepdims=True))
        alpha = jnp.exp(m - m_new)
        p = jnp.exp(s - m_new)
        l = alpha * l + jnp.sum(p, axis=1, keepdims=True)
        acc = alpha * acc + _dot(p.astype(BF16), vblk)
        return m_new, l, acc

    carry = (jnp.full((2 * tq, 1), NEG_INF, F32), jnp.zeros((2 * tq, 1), F32), jnp.zeros((2 * tq, ATT_DV), F32))
    carry = step(carry, kc_ref[...], vc_ref[...])

    def body(j, carry):
        off = pl.multiple_of(j * tk, tk)
        return step(carry, kx_ref[pl.ds(off, tk), :], vx_ref[pl.ds(off, tk), :])

    m, l, acc = lax.fori_loop(0, kx_ref.shape[0] // tk, body, carry, unroll=True)
    o = acc / l
    lam = (jnp.exp(jnp.sum(lq1_ref[...] * lk1_ref[...], axis=1, keepdims=True))
           - jnp.exp(jnp.sum(lq2_ref[...] * lk2_ref[...], axis=1, keepdims=True)) + LAMBDA_INIT)
    o = o[:tq] - lam * o[tq:]
    ms = jnp.mean(o * o, axis=-1, keepdims=True)
    o = o * lax.rsqrt(ms + NORM_EPS) * sg_ref[...] * (1.0 - LAMBDA_INIT)
    o_ref[...] = o.astype(o_ref.dtype)


def _attention(q, kc, vc, kx, vx, lq1, lk1, lq2, lk2, subln_g, tq, tk):
    b, l, d = q.shape
    nc = kc.shape[1]
    vec = lambda n: pl.BlockSpec((1, n), lambda bi, h, i: (0, 0))
    return pl.pallas_call(
        functools.partial(_attn_kernel, tq=tq, tk=tk),
        grid=(b, ATT_HEADS, l // tq),
        in_specs=[pl.BlockSpec((None, tq, LANES), lambda bi, h, i: (bi, i, h)),
                  pl.BlockSpec((None, nc, LANES), lambda bi, h, i: (bi, 0, h)),
                  pl.BlockSpec((None, nc, LANES), lambda bi, h, i: (bi, 0, h)),
                  pl.BlockSpec((None, l, LANES), lambda bi, h, i: (bi, 0, h)),
                  pl.BlockSpec((None, l, LANES), lambda bi, h, i: (bi, 0, h)),
                  vec(ATT_DK), vec(ATT_DK), vec(ATT_DK), vec(ATT_DK), vec(ATT_DV)],
        out_specs=pl.BlockSpec((None, tq, LANES), lambda bi, h, i: (bi, i, h)),
        out_shape=jax.ShapeDtypeStruct((b, l, d), BF16),
        compiler_params=_params(("arbitrary", "arbitrary", "arbitrary")),
        name="diff_attn",
    )(q, kc, vc, kx, vx, lq1, lk1, lq2, lk2, subln_g)


def _ssm_weights(a_re, a_im, log_dt, b_re, b_im, c_re, c_im, d_skip, nsteps):
    ch, g, n, p = SSM_CHUNK, SSM_GROUPS, SSM_STATE, SSM_GROUP
    ar = a_re.astype(F32)
    ai = a_im.astype(F32)
    dt = jnp.exp(log_dt.astype(F32))[..., None]
    lam = ar * dt
    theta = ai * dt
    mag = jnp.exp(lam)
    abar_re = mag * jnp.cos(theta)
    abar_im = mag * jnp.sin(theta)
    den = ar * ar + ai * ai
    nr = abar_re - 1.0
    ni = abar_im
    fr = (nr * ar + ni * ai) / den
    fi = (ni * ar - nr * ai) / den
    br = b_re.astype(F32)
    bi = b_im.astype(F32)
    bb_re = fr[..., None] * br - fi[..., None] * bi
    bb_im = fr[..., None] * bi + fi[..., None] * br
    cr = c_re.astype(F32)
    ci = c_im.astype(F32)

    j = jnp.arange(ch + 1, dtype=F32)[:, None, None, None]
    pr = jnp.exp(lam[None] * j) * jnp.cos(theta[None] * j)
    pi = jnp.exp(lam[None] * j) * jnp.sin(theta[None] * j)

    cb_re = cr[..., None] * bb_re[:, :, None] - ci[..., None] * bb_im[:, :, None]
    cb_im = cr[..., None] * bb_im[:, :, None] + ci[..., None] * bb_re[:, :, None]
    kker = (jnp.einsum("dgpnq,ldgn->dlgpq", cb_re, pr[:ch], precision=HIGHEST)
            - jnp.einsum("dgpnq,ldgn->dlgpq", cb_im, pi[:ch], precision=HIGHEST))
    ii = jnp.arange(ch)[:, None]
    jj = jnp.arange(ch)[None, :]
    kf = jnp.where((jj >= ii)[..., None, None, None], kker[0][jnp.clip(jj - ii, 0, ch - 1)], 0.0)
    kb = jnp.where((ii >= jj)[..., None, None, None], kker[1][jnp.clip(ii - jj, 0, ch - 1)], 0.0)
    mm = jnp.transpose(kf + kb, (2, 0, 4, 1, 3)).reshape(g, ch * p, ch * p)
    dvec = jnp.tile(d_skip.astype(F32).reshape(g, 1, p), (1, ch, 1)).reshape(g, ch * p)
    mm = mm + jnp.eye(ch * p, dtype=F32)[None] * dvec[:, None, :]

    def bpow(d, idx):
        pre = pr[idx, d][:, :, :, None]
        pim = pi[idx, d][:, :, :, None]
        re = pre * bb_re[d][None] - pim * bb_im[d][None]
        im = pre * bb_im[d][None] + pim * bb_re[d][None]
        both = jnp.stack([re, im], axis=0)
        return jnp.transpose(both, (2, 1, 4, 0, 3)).reshape(g, ch * p, 2 * n)

    def cpow(d, idx):
        pre = pr[idx, d][:, :, None, :]
        pim = pi[idx, d][:, :, None, :]
        ca_re = cr[d][None] * pre - ci[d][None] * pim
        ca_im = cr[d][None] * pim + ci[d][None] * pre
        both = jnp.stack([ca_re, -ca_im], axis=0)
        return jnp.transpose(both, (2, 0, 4, 1, 3)).reshape(g, 2 * n, ch * p)

    ar_idx = jnp.arange(ch)
    bpf = bpow(0, ch - 1 - ar_idx)
    bpb = bpow(1, ar_idx)
    cpf = cpow(0, ar_idx + 1)
    cpb = cpow(1, ch - ar_idx)

    re, im = pr[ch], pi[ch]
    rows1, rows2 = [], []
    for _ in range(nsteps):
        rows1.append(jnp.concatenate([re, re], axis=-1))
        rows2.append(jnp.concatenate([-im, im], axis=-1))
        re, im = re * re - im * im, 2.0 * re * im
    pad = [jnp.zeros_like(rows1[0])] * (16 - nsteps)
    coef = jnp.stack([jnp.stack(rows1 + pad, axis=2), jnp.stack(rows2 + pad, axis=2)], axis=2)
    coef = jnp.transpose(coef, (1, 0, 2, 3, 4)).reshape(g, 4, 16, 2 * n)
    return mm, bpf, bpb, cpf, cpb, coef


def _ssm_kernel(ux_ref, uc_ref, m_ref, bpf_ref, bpb_ref, cpf_ref, cpb_ref, coef_ref, y_ref, *, nb, nsteps):
    ux = ux_ref[...]
    uc = uc_ref[...]
    rx, rc = ux.shape[0], uc.shape[0]
    r = rx + rc
    bpf = bpf_ref[...]
    bpb = bpb_ref[...]
    hf = jnp.concatenate([_dot(uc, bpf, HIGHEST), _dot(ux, bpf, HIGHEST)], axis=0)
    gb = jnp.concatenate([_dot(ux, bpb, HIGHEST), _dot(uc, bpb, HIGHEST)], axis=0)
    row = lax.broadcasted_iota(jnp.int32, (r, 1), 0)
    half = SSM_STATE
    for k in range(nsteps):
        s = nb << k
        if s >= r:
            break
        sh = jnp.where(row >= s, pltpu.roll(hf, s, 0), 0.0)
        hf = hf + coef_ref[0, k:k + 1, :] * sh + coef_ref[1, k:k + 1, :] * pltpu.roll(sh, half, 1)
        sh = jnp.where(row < r - s, pltpu.roll(gb, r - s, 0), 0.0)
        gb = gb + coef_ref[2, k:k + 1, :] * sh + coef_ref[3, k:k + 1, :] * pltpu.roll(sh, half, 1)
    hprev = jnp.where(row >= nb, pltpu.roll(hf, nb, 0), 0.0)[rc:]
    gnext = jnp.where(row < r - nb, pltpu.roll(gb, r - nb, 0), 0.0)[:rx]
    y_ref[...] = (_dot(ux, m_ref[...], HIGHEST) + _dot(hprev, cpf_ref[...], HIGHEST)
                  + _dot(gnext, cpb_ref[...], HIGHEST))


def _ssm(ux, uc, weights, nb, nsteps):
    g, rx, w = ux.shape
    rc = uc.shape[1]
    mm, bpf, bpb, cpf, cpb, coef = weights
    per_g = lambda a: pl.BlockSpec((None,) + a.shape[1:], lambda gi: (gi,) + (0,) * (a.ndim - 1))
    return pl.pallas_call(
        functools.partial(_ssm_kernel, nb=nb, nsteps=nsteps),
        grid=(g,),
        in_specs=[per_g(ux), per_g(uc), per_g(mm), per_g(bpf), per_g(bpb), per_g(cpf), per_g(cpb), per_g(coef)],
        out_specs=pl.BlockSpec((None, rx, w), lambda gi: (gi, 0, 0)),
        out_shape=jax.ShapeDtypeStruct((g, rx, w), F32),
        compiler_params=_params(("arbitrary",)),
        name="s5_chunked",
    )(ux, uc, mm, bpf, bpb, cpf, cpb, coef)


def _merge_kernel(x_ref, attn_ref, y_ref, sg_ref, mod_ref, g2_ref, wglu_ref, bglu_ref, wau_ref, wsu_ref, wout_ref,
                  wqry_ref, k1_ref, k2_ref, x1_ref, xn_ref, s1_ref, s2_ref):
    d = D_MODEL
    yg = _gelu_tanh(y_ref[...])
    ysg = yg * _sigmoid(_dot(yg.astype(BF16), wglu_ref[...]) + bglu_ref[...])
    a_up = _dot(attn_ref[...], wau_ref[...])
    s_up = _dot(ysg.astype(BF16), wsu_ref[...])
    mix_in = sg_ref[:, :d].astype(F32) * a_up + sg_ref[:, d:].astype(F32) * s_up
    mix = _dot(mix_in.astype(BF16), wout_ref[...])
    x1 = x_ref[...] + mod_ref[0:1, :] * mix
    x1_ref[...] = x1
    xn = _norm_mod(x1, g2_ref[...], mod_ref[1:2, :], mod_ref[2:3, :]).astype(BF16)
    xn_ref[...] = xn
    qb = _dot(xn, wqry_ref[...]).astype(BF16)
    for h in range(PEER_HEADS):
        qh = qb[:, h * LANES:(h + 1) * LANES]
        s1_ref[h * LANES:(h + 1) * LANES, :] = _dot_nt(k1_ref[...], qh)
        s2_ref[h * LANES:(h + 1) * LANES, :] = _dot_nt(k2_ref[...], qh)


def _merge(x, attn, y, sg, mod, g2, wglu, bglu, wau, wsu, wout, wqry, k1p, k2p, tm):
    b, l, d = x.shape
    ntok = b * l
    nt = l // tm
    tok = lambda n: pl.BlockSpec((None, tm, n), lambda bi, i: (bi, i, 0))
    tokT = pl.BlockSpec((PEER_HEADS * PEER_NKEYS, tm), lambda bi, i: (0, bi * nt + i))
    return pl.pallas_call(
        _merge_kernel,
        grid=(b, nt),
        in_specs=[tok(d), tok(d), tok(SSM_W), tok(2 * d),
                  pl.BlockSpec((None, 8, d), lambda bi, i: (bi, 0, 0)),
                  pl.BlockSpec((1, d), lambda bi, i: (0, 0)),
                  _const_spec(wglu.shape), _const_spec(bglu.shape), _const_spec(wau.shape),
                  _const_spec(wsu.shape), _const_spec(wout.shape), _const_spec(wqry.shape),
                  _const_spec(k1p.shape), _const_spec(k2p.shape)],
        out_specs=[tok(d), tok(d), tokT, tokT],
        out_shape=[jax.ShapeDtypeStruct((b, l, d), F32),
                   jax.ShapeDtypeStruct((b, l, d), BF16),
                   jax.ShapeDtypeStruct((PEER_HEADS * PEER_NKEYS, ntok), F32),
                   jax.ShapeDtypeStruct((PEER_HEADS * PEER_NKEYS, ntok), F32)],
        compiler_params=_params(("arbitrary", "arbitrary")),
        name="merge_peer_query",
    )(x, attn, y, sg, mod, g2, wglu, bglu, wau, wsu, wout, wqry, k1p, k2p)


CAND_ROWS = 80


def _pair_candidates(v1, v2):
    rows8 = lax.broadcasted_iota(jnp.int32, (8, 1), 0)
    blocks = [v1[0:1, :] + v2]
    for a in range(1, 8):
        blocks.append(jnp.where(rows8 < PEER_TOPK // (a + 1), v1[a:a + 1, :] + v2[0:8, :], NEG_INF))
    blocks.append(v1[8:16, :] + v2[0:1, :])
    return jnp.concatenate(blocks, axis=0)


def _route_kernel(s1_ref, s2_ref, nsel_ref, coef_ref, rank2_ref, e2_ref,
                  c1_ref, c2_ref, rk_ref, v1_ref, v2_ref, cand_ref, tau_ref):
    nh, nk, topk = PEER_HEADS, PEER_NKEYS, PEER_TOPK
    keys = lambda h: slice(h * nk, (h + 1) * nk)
    tops = lambda h: slice(h * topk, (h + 1) * topk)
    cands = lambda h: slice(h * CAND_ROWS, (h + 1) * CAND_ROWS)
    c1_ref[...] = s1_ref[...]
    c2_ref[...] = s2_ref[...]
    rk_ref[...] = jnp.full(rk_ref.shape, float(topk), F32)
    v1_ref[...] = jnp.zeros_like(v1_ref)
    v2_ref[...] = jnp.zeros_like(v2_ref)
    rows16 = lax.broadcasted_iota(jnp.int32, (topk, 1), 0)

    def extract(k, _):
        for h in range(nh):
            cur = c1_ref[keys(h), :]
            m = jnp.max(cur, axis=0, keepdims=True)
            c1_ref[keys(h), :] = jnp.where(cur >= m, NEG_INF, cur)
            v1_ref[tops(h), :] = jnp.where(rows16 == k, m, v1_ref[tops(h), :])
            cur = c2_ref[keys(h), :]
            m = jnp.max(cur, axis=0, keepdims=True)
            hit = cur >= m
            c2_ref[keys(h), :] = jnp.where(hit, NEG_INF, cur)
            rk_ref[keys(h), :] = jnp.where(hit, k.astype(F32), rk_ref[keys(h), :])
            v2_ref[tops(h), :] = jnp.where(rows16 == k, m, v2_ref[tops(h), :])
        return 0

    lax.fori_loop(0, topk, extract, 0)

    for h in range(nh):
        cand_ref[cands(h), :] = _pair_candidates(v1_ref[tops(h), :], v2_ref[tops(h), :])

    def knock(k, _):
        for h in range(nh):
            cur = cand_ref[cands(h), :]
            m = jnp.max(cur, axis=0, keepdims=True)
            cand_ref[cands(h), :] = jnp.where(cur >= m, NEG_INF, cur)
            tau_ref[h:h + 1, :] = m
        return 0

    lax.fori_loop(0, topk, knock, 0)

    for h in range(nh):
        v1 = v1_ref[tops(h), :]
        v2 = v2_ref[tops(h), :]
        tau = tau_ref[h:h + 1, :]
        cand = _pair_candidates(v1, v2)
        mx = v1[0:1, :] + v2[0:1, :]
        z = jnp.sum(jnp.where(cand >= tau, jnp.exp(cand - mx), 0.0), axis=0, keepdims=True)
        s1 = s1_ref[keys(h), :]
        nsel = jnp.zeros_like(s1)
        for bpos in range(topk):
            nsel = nsel + jnp.where(s1 + v2[bpos:bpos + 1, :] >= tau, 1.0, 0.0)
        nsel_ref[keys(h), :] = jnp.where(s1 >= v1[topk - 1:topk, :], nsel, 0.0)
        coef_ref[keys(h), :] = jnp.exp(s1 - v1[0:1, :]) / z
        rank2_ref[keys(h), :] = rk_ref[keys(h), :].astype(rank2_ref.dtype)
        e2_ref[keys(h), :] = jnp.exp(s2_ref[keys(h), :] - v2[0:1, :]).astype(e2_ref.dtype)


def _route(s1t, s2t, tt):
    rows, ntok = s1t.shape
    spec = pl.BlockSpec((rows, tt), lambda i: (0, i))
    f32 = jax.ShapeDtypeStruct((rows, ntok), F32)
    big = pltpu.VMEM((rows, tt), F32)
    small = pltpu.VMEM((PEER_HEADS * PEER_TOPK, tt), F32)
    return pl.pallas_call(
        _route_kernel,
        grid=(ntok // tt,),
        in_specs=[spec, spec],
        out_specs=[spec, spec, spec, spec],
        out_shape=[f32, f32, f32, f32],
        scratch_shapes=[big, big, big, small, small,
                        pltpu.VMEM((PEER_HEADS * CAND_ROWS, tt), F32), pltpu.VMEM((PEER_HEADS, tt), F32)],
        compiler_params=_params(("arbitrary",)),
        name="peer_route",
    )(s1t, s2t)


BF16_ROWS = 16


def _peer_step(chunk, xn_ref, u_ref, vt_ref, rank2_ref, e2_ref, nsel_ref, coef_ref,
               act_next_ref, act_ref, gate_ref, acc_ref, *, te, tt):
    act_next_ref[...] = _dot_nt(u_ref[...], xn_ref[...])
    nsub = te // PEER_NKEYS
    base = chunk * BF16_ROWS
    i1s = [pl.ds(pl.multiple_of(h * (PEER_NKEYS // nsub) * BF16_ROWS + base, BF16_ROWS), BF16_ROWS)
           for h in range(PEER_HEADS)]
    zero = jnp.zeros((PEER_NKEYS, LANES), BF16)
    for lb in range(tt // LANES):
        ls = slice(lb * LANES, (lb + 1) * LANES)
        ns = [nsel_ref[i1s[h], ls] for h in range(PEER_HEADS)]
        cf = [coef_ref[i1s[h], ls] for h in range(PEER_HEADS)]
        for r in range(nsub):
            rs = slice(r * PEER_NKEYS, (r + 1) * PEER_NKEYS)
            w = zero
            for h in range(PEER_HEADS):
                hs = slice(h * PEER_NKEYS, (h + 1) * PEER_NKEYS)
                sel = rank2_ref[hs, ls] < ns[h][r:r + 1, :]
                w = w + jnp.where(sel, e2_ref[hs, ls], zero) * cf[h][r:r + 1, :]
            gate_ref[rs, ls] = (w.astype(F32) * _gelu_tanh(act_ref[rs, ls])).astype(BF16)
    acc_ref[...] += _dot(vt_ref[...], gate_ref[...])
    act_ref[...] = act_next_ref[...]


def _peer_kernel(xn_ref, u_ref, vt_ref, rank2_ref, e2_ref, nsel_ref, coef_ref, x1_ref, mod_ref, gf_ref, o_ref,
                 act_ref, act_next_ref, gate_ref, acc_ref, rank2b_ref, e2b_ref, nselb_ref, coefb_ref, *, te, tt):
    nsub = te // PEER_NKEYS
    assert nsub <= BF16_ROWS
    j = pl.program_id(2)
    last = pl.num_programs(2) - 1

    @pl.when(j == 0)
    def _():
        act_ref[...] = _dot_nt(u_ref[...], xn_ref[...])
        acc_ref[...] = jnp.zeros_like(acc_ref)
        rank2b_ref[...] = rank2_ref[...].astype(BF16)
        e2b_ref[...] = e2_ref[...].astype(BF16)
        nselb_ref[...] = jnp.zeros_like(nselb_ref)
        coefb_ref[...] = jnp.zeros_like(coefb_ref)
        for g in range(PEER_HEADS * PEER_NKEYS // nsub):
            src = slice(g * nsub, (g + 1) * nsub)
            dst = slice(g * BF16_ROWS, g * BF16_ROWS + nsub)
            nselb_ref[dst, :] = nsel_ref[src, :].astype(BF16)
            coefb_ref[dst, :] = coef_ref[src, :].astype(BF16)

    @pl.when(j > 0)
    def _():
        _peer_step(j - 1, xn_ref, u_ref, vt_ref, rank2b_ref, e2b_ref, nselb_ref, coefb_ref,
                   act_next_ref, act_ref, gate_ref, acc_ref, te=te, tt=tt)

    @pl.when(j == last)
    def _():
        x2 = x1_ref[...] + mod_ref[0:1, :] * acc_ref[...].T
        ms = jnp.mean(x2 * x2, axis=-1, keepdims=True)
        o_ref[...] = x2 * lax.rsqrt(ms + NORM_EPS) * gf_ref[...]


def _peer(xn, u, vt, rank2, e2t, nsel, coef, x1, mod, gf, tt, te):
    b, l, d = x1.shape
    nt = l // tt
    ne = PEER_EXPERTS // te
    rows = PEER_HEADS * PEER_NKEYS
    tok = pl.BlockSpec((None, tt, d), lambda bi, i, j: (bi, i, 0))
    tokT = pl.BlockSpec((rows, tt), lambda bi, i, j: (0, bi * nt + i))
    return pl.pallas_call(
        functools.partial(_peer_kernel, te=te, tt=tt),
        grid=(b, nt, ne + 1),
        in_specs=[tok,
                  pl.BlockSpec((te, d), lambda bi, i, j: (jnp.minimum(j, ne - 1), 0)),
                  pl.BlockSpec((d, te), lambda bi, i, j: (0, jnp.maximum(j - 1, 0))),
                  tokT, tokT, tokT, tokT, tok,
                  pl.BlockSpec((None, 8, d), lambda bi, i, j: (bi, 0, 0)),
                  pl.BlockSpec((1, d), lambda bi, i, j: (0, 0))],
        out_specs=tok,
        out_shape=jax.ShapeDtypeStruct((b, l, d), F32),
        scratch_shapes=[pltpu.VMEM((te, tt), F32), pltpu.VMEM((te, tt), F32), pltpu.VMEM((te, tt), BF16),
                        pltpu.VMEM((d, tt), F32), pltpu.VMEM((rows, tt), BF16), pltpu.VMEM((rows, tt), BF16),
                        pltpu.VMEM((rows * BF16_ROWS // (te // PEER_NKEYS), tt), BF16),
                        pltpu.VMEM((rows * BF16_ROWS // (te // PEER_NKEYS), tt), BF16)],
        compiler_params=_params(("arbitrary", "arbitrary", "arbitrary")),
        name="peer_dense",
    )(xn, u, vt, rank2, e2t, nsel, coef, x1, mod, gf)


def _rope_tables(l):
    n_freq = ATT_DK // 4
    inv = ROPE_BASE ** (-2.0 * jnp.arange(n_freq, dtype=F32) / (ATT_DK // 2))
    t = jnp.arange(l, dtype=jnp.int32)
    ang_r = (t // GRID_W).astype(F32)[:, None] * inv[None, :]
    ang_c = (t % GRID_W).astype(F32)[:, None] * inv[None, :]
    cos = jnp.concatenate([jnp.cos(ang_r)] * 2 + [jnp.cos(ang_c)] * 2, axis=-1)
    sin = jnp.concatenate([-jnp.sin(ang_r), jnp.sin(ang_r), -jnp.sin(ang_c), jnp.sin(ang_c)], axis=-1)
    return jnp.tile(cos, (1, 2)), jnp.tile(sin, (1, 2))


def _pick(n, pref):
    for t in pref:
        if n % t == 0:
            return t
    return n


def kernel(x, c, ctx, c_ctx, ada_w, ada_b, norm1_g, norm2_g, w_in, lambda_q1, lambda_k1, lambda_q2, lambda_k2, subln_g, w_attn_up, ssm_a_re, ssm_a_im, ssm_log_dt, ssm_b_re, ssm_b_im, ssm_c_re, ssm_c_im, ssm_d, w_glu, b_glu, w_ssm_up, w_out, peer_w_query, peer_sub_k1, peer_sub_k2, peer_u, peer_v, final_norm_g):
    assert ada_w.shape[0] == 1, "single-layer block"
    b, l, d = x.shape
    nc = ctx.shape[1]
    assert d == D_MODEL and b + 1 <= 8
    ch = SSM_CHUNK

    cin = jnp.concatenate([c, c_ctx[None, :], jnp.zeros((7 - b, d), F32)], axis=0)
    mod = _adaln(cin, ada_w[0], ada_b[0][None, :])
    sh1, sc1, g1, sh2, sc2, g2 = [mod[:, k * d:(k + 1) * d] for k in range(6)]
    mod1 = jnp.stack([sh1[:b], sc1[:b]], axis=1)
    mod1c = jnp.stack([sh1[b], sc1[b]], axis=0)
    zpad = jnp.zeros((b, 5, d), F32)
    mod2 = jnp.concatenate([jnp.stack([g1[:b], sh2[:b], sc2[:b]], axis=1), zpad], axis=1)
    mod3 = jnp.concatenate([g2[:b, None, :], jnp.zeros((b, 7, d), F32)], axis=1)

    w = w_in[0]
    n_q = 2 * ATT_HEADS * ATT_DK
    o_k, o_v, o_u, o_g = n_q, 2 * n_q, 2 * n_q + ATT_HEADS * ATT_DV, 2 * n_q + ATT_HEADS * ATT_DV + SSM_W
    wq, wk, wv, wu, wg = [w[:, a:e].astype(BF16) for a, e in
                          ((0, o_k), (o_k, o_v), (o_v, o_u), (o_u, o_g), (o_g, w.shape[1]))]
    cos, sin = _rope_tables(l)
    g1n = norm1_g[0][None, :]

    tm = _pick(l, (512, 256, 128))
    q, k, v, u, sg = _inproj_x(x, mod1, g1n, cos, sin, wq, wk, wv, wu, wg, tm)
    kc, vc, uc = _inproj_c(ctx, mod1c, g1n, wk, wv, wu)

    row = lambda a: a[0][None, :].astype(F32)
    attn = _attention(q, kc, vc, k, v, row(lambda_q1), row(lambda_k1), row(lambda_q2), row(lambda_k2),
                      row(subln_g), _pick(l, (256, 128)), _pick(l, (512, 256, 128)))

    nsteps = max(1, math.ceil(math.log2((l + nc) // ch)))
    sw = _ssm_weights(ssm_a_re[0], ssm_a_im[0], ssm_log_dt[0], ssm_b_re[0], ssm_b_im[0],
                      ssm_c_re[0], ssm_c_im[0], ssm_d[0], nsteps)
    to_groups = lambda a, n: jnp.transpose(a.reshape(b, n // ch, ch, SSM_GROUPS, SSM_GROUP),
                                           (3, 1, 0, 2, 4)).reshape(SSM_GROUPS, (n // ch) * b, ch * SSM_GROUP)
    yg = _ssm(to_groups(u, l), to_groups(uc, nc), sw, b, nsteps)
    y = jnp.transpose(yg.reshape(SSM_GROUPS, l // ch, b, ch, SSM_GROUP), (2, 1, 3, 0, 4)).reshape(b, l, SSM_W)

    half = PEER_DKEY // 2
    zk = jnp.zeros((PEER_NKEYS, half), F32)
    k1p = jnp.concatenate([peer_sub_k1[0], zk], axis=1).astype(BF16)
    k2p = jnp.concatenate([zk, peer_sub_k2[0]], axis=1).astype(BF16)
    x1, xn2, s1t, s2t = _merge(x, attn, y, sg, mod2, norm2_g[0][None, :],
                               w_glu[0].astype(BF16), b_glu[0][None, :], w_attn_up[0].astype(BF16),
                               w_ssm_up[0].astype(BF16), w_out[0].astype(BF16), peer_w_query[0].astype(BF16),
                               k1p, k2p, tm)

    nsel, coef, rank2, e2t = _route(s1t, s2t, _pick(b * l, (256, 128)))
    tt = _pick(l, (512, 256, 128))
    out = _peer(xn2, peer_u[0].astype(BF16), peer_v[0].T.astype(BF16), rank2, e2t, nsel, coef, x1, mod3,
                final_norm_g[None, :], tt, 1024)
    return out
```

```python
import functools
import math

import jax
import jax.numpy as jnp
from jax import lax
from jax.experimental import pallas as pl
from jax.experimental.pallas import tpu as pltpu

F32 = jnp.float32
BF16 = jnp.bfloat16
HIGHEST = lax.Precision.HIGHEST

LANES = 128
MXU_WIDTH = 256
VMEM_LIMIT = 56 * 1024 * 1024

D_MODEL = 1024
NORM_EPS = 1e-6
GRID_W = 64
ATT_HEADS = 8
ATT_DK = 64
ATT_DV = 2 * ATT_DK
ATT_Q_SCALE = ATT_DK ** -0.5 * math.log2(math.e)
ROPE_BASE = 10000.0
SSM_GROUP = 16
SSM_W = D_MODEL // 2
SSM_GROUPS = SSM_W // SSM_GROUP
SSM_STATE = 64
SSM_CHUNK = 16
PEER_HEADS = 8
PEER_NKEYS = 128
PEER_EXPERTS = PEER_NKEYS * PEER_NKEYS
PEER_TOPK = 16
PEER_DKEY = 128
LAMBDA_INIT = 0.8 - 0.6 * math.exp(-0.3 * 0)

NEG_INF = float("-inf")


def _dot(a, b, precision=None):
    return jnp.dot(a, b, preferred_element_type=F32, precision=precision)


def _dot_nt(a, b):
    return lax.dot_general(a, b, (((1,), (1,)), ((), ())), preferred_element_type=F32)


def _sigmoid(z):
    return 1.0 / (1.0 + jnp.exp(-z))


def _gelu_tanh(z):
    return 0.5 * z * (1.0 + jnp.tanh(math.sqrt(2.0 / math.pi) * (z + 0.044715 * (z * z * z))))


def _norm_mod(xf, g, shift, scale):
    ms = jnp.mean(xf * xf, axis=-1, keepdims=True)
    xn = xf * lax.rsqrt(ms + NORM_EPS) * g
    return xn * (1.0 + scale) + shift


def _params(sem):
    return pltpu.CompilerParams(dimension_semantics=sem, vmem_limit_bytes=VMEM_LIMIT)


def _const_spec(shape):
    nd = len(shape)
    return pl.BlockSpec(shape, lambda *_: (0,) * nd, pipeline_mode=pl.Buffered(1))


def _adaln_kernel(c_ref, w_ref, b_ref, o_ref):
    a = c_ref[...]
    act = a * _sigmoid(a)
    o_ref[...] = _dot(act, w_ref[...], precision=HIGHEST) + b_ref[...]


def _adaln(cin, w, b):
    n = w.shape[1]
    tn = 1536
    return pl.pallas_call(
        _adaln_kernel,
        grid=(n // tn,),
        in_specs=[pl.BlockSpec((8, D_MODEL), lambda j: (0, 0)),
                  pl.BlockSpec((D_MODEL, tn), lambda j: (0, j)),
                  pl.BlockSpec((1, tn), lambda j: (0, j))],
        out_specs=pl.BlockSpec((8, tn), lambda j: (0, j)),
        out_shape=jax.ShapeDtypeStruct((8, n), F32),
        compiler_params=_params(("arbitrary",)),
        name="adaln",
    )(cin, w, b)


def _rope_store(h, cos, sin, low, scale, o_ref):
    for hh in range(ATT_HEADS):
        blk = h[:, hh * LANES:(hh + 1) * LANES]
        sw = jnp.where(low, pltpu.roll(blk, LANES - 16, 1), pltpu.roll(blk, 16, 1))
        o_ref[:, hh * LANES:(hh + 1) * LANES] = ((blk * cos + sw * sin) * scale).astype(o_ref.dtype)


def _inproj_x_kernel(x_ref, mod_ref, g_ref, cos_ref, sin_ref, wq_ref, wk_ref, wv_ref, wu_ref, wg_ref,
                     q_ref, k_ref, v_ref, u_ref, sg_ref):
    xb = _norm_mod(x_ref[...], g_ref[...], mod_ref[0:1, :], mod_ref[1:2, :]).astype(BF16)
    cos = cos_ref[...]
    sin = sin_ref[...]
    lane = lax.broadcasted_iota(jnp.int32, (1, LANES), 1)
    low = (lane & 31) < 16
    _rope_store(_dot(xb, wq_ref[...]), cos, sin, low, ATT_Q_SCALE, q_ref)
    _rope_store(_dot(xb, wk_ref[...]), cos, sin, low, 1.0, k_ref)
    v_ref[...] = _dot(xb, wv_ref[...]).astype(v_ref.dtype)
    u_ref[...] = _dot(xb, wu_ref[...])
    sg_ref[...] = _sigmoid(_dot(xb, wg_ref[...])).astype(sg_ref.dtype)


def _inproj_x(x, mod, g, cos, sin, wq, wk, wv, wu, wg, tm):
    b, l, d = x.shape
    tok = lambda n: pl.BlockSpec((None, tm, n), lambda bi, i: (bi, i, 0))
    return pl.pallas_call(
        _inproj_x_kernel,
        grid=(b, l // tm),
        in_specs=[tok(d),
                  pl.BlockSpec((None, 2, d), lambda bi, i: (bi, 0, 0)),
                  pl.BlockSpec((1, d), lambda bi, i: (0, 0)),
                  pl.BlockSpec((tm, LANES), lambda bi, i: (i, 0)),
                  pl.BlockSpec((tm, LANES), lambda bi, i: (i, 0)),
                  _const_spec(wq.shape), _const_spec(wk.shape), _const_spec(wv.shape),
                  _const_spec(wu.shape), _const_spec(wg.shape)],
        out_specs=[tok(d), tok(d), tok(d), tok(SSM_W), tok(2 * d)],
        out_shape=[jax.ShapeDtypeStruct((b, l, d), BF16),
                   jax.ShapeDtypeStruct((b, l, d), BF16),
                   jax.ShapeDtypeStruct((b, l, d), BF16),
                   jax.ShapeDtypeStruct((b, l, SSM_W), F32),
                   jax.ShapeDtypeStruct((b, l, 2 * d), BF16)],
        compiler_params=_params(("arbitrary", "arbitrary")),
        name="inproj_x",
    )(x, mod, g, cos, sin, wq, wk, wv, wu, wg)


def _inproj_c_kernel(x_ref, mod_ref, g_ref, wk_ref, wv_ref, wu_ref, k_ref, v_ref, u_ref):
    xb = _norm_mod(x_ref[...], g_ref[...], mod_ref[0:1, :], mod_ref[1:2, :]).astype(BF16)
    k_ref[...] = _dot(xb, wk_ref[...]).astype(k_ref.dtype)
    v_ref[...] = _dot(xb, wv_ref[...]).astype(v_ref.dtype)
    u_ref[...] = _dot(xb, wu_ref[...])


def _inproj_c(ctx, mod, g, wk, wv, wu):
    b, n, d = ctx.shape
    tok = lambda w: pl.BlockSpec((None, n, w), lambda bi: (bi, 0, 0))
    return pl.pallas_call(
        _inproj_c_kernel,
        grid=(b,),
        in_specs=[tok(d),
                  pl.BlockSpec((2, d), lambda bi: (0, 0)),
                  pl.BlockSpec((1, d), lambda bi: (0, 0)),
                  _const_spec(wk.shape), _const_spec(wv.shape), _const_spec(wu.shape)],
        out_specs=[tok(d), tok(d), tok(SSM_W)],
        out_shape=[jax.ShapeDtypeStruct((b, n, d), BF16),
                   jax.ShapeDtypeStruct((b, n, d), BF16),
                   jax.ShapeDtypeStruct((b, n, SSM_W), F32)],
        compiler_params=_params(("arbitrary",)),
        name="inproj_ctx",
    )(ctx, mod, g, wk, wv, wu)


def _attn_chunks(kc_ref, kx_ref, tk):
    nc, nx = kc_ref.shape[0], kx_ref.shape[0]
    return [(False, 0, nc, 0)] + [(True, j * tk, tk, nc + j * tk) for j in range(nx // tk)]


def _attn_step(q_ref, kc_ref, vc_ref, kx_ref, vx_ref, lq1_ref, lk1_ref, lq2_ref, lk2_ref, sg_ref, o_ref,
               s_new_ref, m_new_ref, s_old_ref, m_old_ref, *, tq, tk, score, finish):
    if score:
        q = q_ref[...]
        lane = lax.broadcasted_iota(jnp.int32, (1, LANES), 1)
        zero = jnp.zeros_like(q)
        qs = jnp.concatenate([jnp.where(lane < ATT_DK, q, zero), jnp.where(lane >= ATT_DK, q, zero)], axis=0)
        m_lane = jnp.full((2 * tq, LANES), NEG_INF, F32)
    if finish:
        m = jnp.broadcast_to(jnp.max(m_old_ref[...], axis=1, keepdims=True), (2 * tq, LANES))
        acc = jnp.zeros((2 * tq, 2 * ATT_DV), F32)
    for latent, off, n, col in _attn_chunks(kc_ref, kx_ref, tk):
        k_ref, v_ref = (kx_ref, vx_ref) if latent else (kc_ref, vc_ref)
        if score:
            s = _dot_nt(qs, k_ref[off:off + n, :])
            s_new_ref[:, col:col + n] = s
            smax = s[:, 0:LANES]
            for g in range(1, n // LANES):
                smax = jnp.maximum(smax, s[:, g * LANES:(g + 1) * LANES])
            m_lane = jnp.maximum(m_lane, smax)
        if finish:
            mc = m
            if score:
                mc = m + jnp.minimum(jnp.abs(smax), 0.0)
            ps = [jnp.exp2(s_old_ref[:, col + g * LANES:col + (g + 1) * LANES] - mc).astype(BF16)
                  for g in range(n // LANES)]
            v1 = jnp.concatenate([v_ref[off:off + n, :], jnp.ones((n, ATT_DV), BF16)], axis=1)
            acc = acc + _dot(jnp.concatenate(ps, axis=1), v1)
    if score:
        m_new_ref[...] = m_lane
    if finish:
        o = acc[:, :ATT_DV] / acc[:, ATT_DV:]
        lam = (jnp.exp(jnp.sum(lq1_ref[...] * lk1_ref[...], axis=1, keepdims=True))
               - jnp.exp(jnp.sum(lq2_ref[...] * lk2_ref[...], axis=1, keepdims=True)) + LAMBDA_INIT)
        o = o[:tq] - lam * o[tq:]
        ms = jnp.mean(o * o, axis=-1, keepdims=True)
        o = o * lax.rsqrt(ms + NORM_EPS) * sg_ref[...] * (1.0 - LAMBDA_INIT)
        o_ref[...] = o.astype(o_ref.dtype)


def _attn_kernel(q_ref, kc_ref, vc_ref, kx_ref, vx_ref, lq1_ref, lk1_ref, lq2_ref, lk2_ref, sg_ref, o_ref,
                 s0_ref, s1_ref, m0_ref, m1_ref, *, tq, tk):
    i = pl.program_id(2)
    step = functools.partial(_attn_step, q_ref, kc_ref, vc_ref, kx_ref, vx_ref, lq1_ref, lk1_ref, lq2_ref, lk2_ref,
                             sg_ref, o_ref, tq=tq, tk=tk)

    @pl.when(i == 0)
    def _():
        step(s0_ref, m0_ref, s1_ref, m1_ref, score=True, finish=False)

    @pl.when(i % 2 == 1)
    def _():
        step(s1_ref, m1_ref, s0_ref, m0_ref, score=True, finish=True)

    @pl.when(jnp.logical_and(i > 0, i % 2 == 0))
    def _():
        step(s0_ref, m0_ref, s1_ref, m1_ref, score=True, finish=True)


def _attention(q, kc, vc, kx, vx, lq1, lk1, lq2, lk2, subln_g, tq, tk):
    b, l, d = q.shape
    nc = kc.shape[1]
    nq = l // tq
    vec = lambda n: pl.BlockSpec((1, n), lambda bi, h, i: (0, 0))
    return pl.pallas_call(
        functools.partial(_attn_kernel, tq=tq, tk=tk),
        grid=(b, ATT_HEADS, nq + 1),
        in_specs=[pl.BlockSpec((None, tq, LANES), lambda bi, h, i: (bi, jnp.minimum(i, nq - 1), h)),
                  pl.BlockSpec((None, nc, LANES), lambda bi, h, i: (bi, 0, h)),
                  pl.BlockSpec((None, nc, LANES), lambda bi, h, i: (bi, 0, h)),
                  pl.BlockSpec((None, l, LANES), lambda bi, h, i: (bi, 0, h)),
                  pl.BlockSpec((None, l, LANES), lambda bi, h, i: (bi, 0, h)),
                  vec(ATT_DK), vec(ATT_DK), vec(ATT_DK), vec(ATT_DK), vec(ATT_DV)],
        out_specs=pl.BlockSpec((None, tq, LANES), lambda bi, h, i: (bi, jnp.maximum(i - 1, 0), h)),
        out_shape=jax.ShapeDtypeStruct((b, l, d), BF16),
        scratch_shapes=[pltpu.VMEM((2 * tq, nc + l), F32), pltpu.VMEM((2 * tq, nc + l), F32),
                        pltpu.VMEM((2 * tq, LANES), F32), pltpu.VMEM((2 * tq, LANES), F32)],
        compiler_params=_params(("arbitrary", "arbitrary", "arbitrary")),
        name="diff_attn",
    )(q, kc, vc, kx, vx, lq1, lk1, lq2, lk2, subln_g)


def _ssm_weights(a_re, a_im, log_dt, b_re, b_im, c_re, c_im, d_skip, nsteps):
    ch, g, n, p = SSM_CHUNK, SSM_GROUPS, SSM_STATE, SSM_GROUP
    ar = a_re.astype(F32)
    ai = a_im.astype(F32)
    dt = jnp.exp(log_dt.astype(F32))[..., None]
    lam = ar * dt
    theta = ai * dt
    mag = jnp.exp(lam)
    abar_re = mag * jnp.cos(theta)
    abar_im = mag * jnp.sin(theta)
    den = ar * ar + ai * ai
    nr = abar_re - 1.0
    ni = abar_im
    fr = (nr * ar + ni * ai) / den
    fi = (ni * ar - nr * ai) / den
    br = b_re.astype(F32)
    bi = b_im.astype(F32)
    bb_re = fr[..., None] * br - fi[..., None] * bi
    bb_im = fr[..., None] * bi + fi[..., None] * br
    cr = c_re.astype(F32)
    ci = c_im.astype(F32)

    j = jnp.arange(ch + 1, dtype=F32)[:, None, None, None]
    pr = jnp.exp(lam[None] * j) * jnp.cos(theta[None] * j)
    pi = jnp.exp(lam[None] * j) * jnp.sin(theta[None] * j)

    cb_re = cr[..., None] * bb_re[:, :, None] - ci[..., None] * bb_im[:, :, None]
    cb_im = cr[..., None] * bb_im[:, :, None] + ci[..., None] * bb_re[:, :, None]
    kker = (jnp.einsum("dgpnq,ldgn->dlgpq", cb_re, pr[:ch], precision=HIGHEST)
            - jnp.einsum("dgpnq,ldgn->dlgpq", cb_im, pi[:ch], precision=HIGHEST))
    ii = jnp.arange(ch)[:, None]
    jj = jnp.arange(ch)[None, :]
    kf = jnp.where((jj >= ii)[..., None, None, None], kker[0][jnp.clip(jj - ii, 0, ch - 1)], 0.0)
    kb = jnp.where((ii >= jj)[..., None, None, None], kker[1][jnp.clip(ii - jj, 0, ch - 1)], 0.0)
    mm = jnp.transpose(kf + kb, (2, 0, 4, 1, 3)).reshape(g, ch * p, ch * p)
    dvec = jnp.tile(d_skip.astype(F32).reshape(g, 1, p), (1, ch, 1)).reshape(g, ch * p)
    mm = mm + jnp.eye(ch * p, dtype=F32)[None] * dvec[:, None, :]

    def bpow(d, idx):
        pre = pr[idx, d][:, :, :, None]
        pim = pi[idx, d][:, :, :, None]
        re = pre * bb_re[d][None] - pim * bb_im[d][None]
        im = pre * bb_im[d][None] + pim * bb_re[d][None]
        both = jnp.stack([re, im], axis=0)
        return jnp.transpose(both, (2, 1, 4, 0, 3)).reshape(g, ch * p, 2 * n)

    def cpow(d, idx):
        pre = pr[idx, d][:, :, None, :]
        pim = pi[idx, d][:, :, None, :]
        ca_re = cr[d][None] * pre - ci[d][None] * pim
        ca_im = cr[d][None] * pim + ci[d][None] * pre
        both = jnp.stack([ca_re, -ca_im], axis=0)
        return jnp.transpose(both, (2, 0, 4, 1, 3)).reshape(g, 2 * n, ch * p)

    ar_idx = jnp.arange(ch)
    bpf = bpow(0, ch - 1 - ar_idx)
    bpb = bpow(1, ar_idx)
    cpf = cpow(0, ar_idx + 1)
    cpb = cpow(1, ch - ar_idx)

    re, im = pr[ch], pi[ch]
    rows1, rows2 = [], []
    for _ in range(nsteps):
        rows1.append(jnp.concatenate([re, re], axis=-1))
        rows2.append(jnp.concatenate([-im, im], axis=-1))
        re, im = re * re - im * im, 2.0 * re * im
    pad = [jnp.zeros_like(rows1[0])] * (16 - nsteps)
    coef = jnp.stack([jnp.stack(rows1 + pad, axis=2), jnp.stack(rows2 + pad, axis=2)], axis=2)
    coef = jnp.transpose(coef, (1, 0, 2, 3, 4)).reshape(g, 4, 16, 2 * n)
    return mm, bpf, bpb, cpf, cpb, coef


def _ssm_kernel(ux_ref, uc_ref, m_ref, bpf_ref, bpb_ref, cpf_ref, cpb_ref, coef_ref, y_ref, *, nb, nsteps):
    ux = ux_ref[...]
    uc = uc_ref[...]
    rx, rc = ux.shape[0], uc.shape[0]
    r = rx + rc
    bpf = bpf_ref[...]
    bpb = bpb_ref[...]
    hf = jnp.concatenate([_dot(uc, bpf, HIGHEST), _dot(ux, bpf, HIGHEST)], axis=0)
    gb = jnp.concatenate([_dot(ux, bpb, HIGHEST), _dot(uc, bpb, HIGHEST)], axis=0)
    row = lax.broadcasted_iota(jnp.int32, (r, 1), 0)
    half = SSM_STATE
    for k in range(nsteps):
        s = nb << k
        if s >= r:
            break
        sh = jnp.where(row >= s, pltpu.roll(hf, s, 0), 0.0)
        hf = hf + coef_ref[0, k:k + 1, :] * sh + coef_ref[1, k:k + 1, :] * pltpu.roll(sh, half, 1)
        sh = jnp.where(row < r - s, pltpu.roll(gb, r - s, 0), 0.0)
        gb = gb + coef_ref[2, k:k + 1, :] * sh + coef_ref[3, k:k + 1, :] * pltpu.roll(sh, half, 1)
    hprev = jnp.where(row >= nb, pltpu.roll(hf, nb, 0), 0.0)[rc:]
    gnext = jnp.where(row < r - nb, pltpu.roll(gb, r - nb, 0), 0.0)[:rx]
    y_ref[...] = (_dot(ux, m_ref[...], HIGHEST) + _dot(hprev, cpf_ref[...], HIGHEST)
                  + _dot(gnext, cpb_ref[...], HIGHEST))


def _ssm(ux, uc, weights, nb, nsteps):
    g, rx, w = ux.shape
    rc = uc.shape[1]
    mm, bpf, bpb, cpf, cpb, coef = weights
    per_g = lambda a: pl.BlockSpec((None,) + a.shape[1:], lambda gi: (gi,) + (0,) * (a.ndim - 1))
    return pl.pallas_call(
        functools.partial(_ssm_kernel, nb=nb, nsteps=nsteps),
        grid=(g,),
        in_specs=[per_g(ux), per_g(uc), per_g(mm), per_g(bpf), per_g(bpb), per_g(cpf), per_g(cpb), per_g(coef)],
        out_specs=pl.BlockSpec((None, rx, w), lambda gi: (gi, 0, 0)),
        out_shape=jax.ShapeDtypeStruct((g, rx, w), F32),
        compiler_params=_params(("arbitrary",)),
        name="s5_chunked",
    )(ux, uc, mm, bpf, bpb, cpf, cpb, coef)


def _merge_kernel(x_ref, attn_ref, y_ref, sg_ref, mod_ref, g2_ref, wglu_ref, bglu_ref, wau_ref, wsu_ref, wout_ref,
                  wqry_ref, k1_ref, k2_ref, x1_ref, xn_ref, s1_ref, s2_ref):
    d = D_MODEL
    yg = _gelu_tanh(y_ref[...])
    ysg = yg * _sigmoid(_dot(yg.astype(BF16), wglu_ref[...]) + bglu_ref[...])
    a_up = _dot(attn_ref[...], wau_ref[...])
    s_up = _dot(ysg.astype(BF16), wsu_ref[...])
    mix_in = sg_ref[:, :d].astype(F32) * a_up + sg_ref[:, d:].astype(F32) * s_up
    mix = _dot(mix_in.astype(BF16), wout_ref[...])
    x1 = x_ref[...] + mod_ref[0:1, :] * mix
    x1_ref[...] = x1
    xn = _norm_mod(x1, g2_ref[...], mod_ref[1:2, :], mod_ref[2:3, :]).astype(BF16)
    xn_ref[...] = xn
    qb = _dot(xn, wqry_ref[...]).astype(BF16)
    for h in range(PEER_HEADS):
        qh = qb[:, h * LANES:(h + 1) * LANES]
        s1_ref[h * LANES:(h + 1) * LANES, :] = _dot_nt(k1_ref[...], qh)
        s2_ref[h * LANES:(h + 1) * LANES, :] = _dot_nt(k2_ref[...], qh)


def _merge(x, attn, y, sg, mod, g2, wglu, bglu, wau, wsu, wout, wqry, k1p, k2p, tm):
    b, l, d = x.shape
    ntok = b * l
    nt = l // tm
    tok = lambda n: pl.BlockSpec((None, tm, n), lambda bi, i: (bi, i, 0))
    tokT = pl.BlockSpec((PEER_HEADS * PEER_NKEYS, tm), lambda bi, i: (0, bi * nt + i))
    return pl.pallas_call(
        _merge_kernel,
        grid=(b, nt),
        in_specs=[tok(d), tok(d), tok(SSM_W), tok(2 * d),
                  pl.BlockSpec((None, 8, d), lambda bi, i: (bi, 0, 0)),
                  pl.BlockSpec((1, d), lambda bi, i: (0, 0)),
                  _const_spec(wglu.shape), _const_spec(bglu.shape), _const_spec(wau.shape),
                  _const_spec(wsu.shape), _const_spec(wout.shape), _const_spec(wqry.shape),
                  _const_spec(k1p.shape), _const_spec(k2p.shape)],
        out_specs=[tok(d), tok(d), tokT, tokT],
        out_shape=[jax.ShapeDtypeStruct((b, l, d), F32),
                   jax.ShapeDtypeStruct((b, l, d), BF16),
                   jax.ShapeDtypeStruct((PEER_HEADS * PEER_NKEYS, ntok), F32),
                   jax.ShapeDtypeStruct((PEER_HEADS * PEER_NKEYS, ntok), F32)],
        compiler_params=_params(("arbitrary", "arbitrary")),
        name="merge_peer_query",
    )(x, attn, y, sg, mod, g2, wglu, bglu, wau, wsu, wout, wqry, k1p, k2p)


CAND_ROWS = 80


def _pair_candidates(v1, v2):
    rows8 = lax.broadcasted_iota(jnp.int32, (8, 1), 0)
    blocks = [v1[0:1, :] + v2]
    for a in range(1, 8):
        blocks.append(jnp.where(rows8 < PEER_TOPK // (a + 1), v1[a:a + 1, :] + v2[0:8, :], NEG_INF))
    blocks.append(v1[8:16, :] + v2[0:1, :])
    return jnp.concatenate(blocks, axis=0)


def _route_kernel(s1_ref, s2_ref, nsel_ref, coef_ref, rank2_ref, e2_ref,
                  c1_ref, c2_ref, rk_ref, v1_ref, v2_ref, cand_ref, tau_ref):
    nh, nk, topk = PEER_HEADS, PEER_NKEYS, PEER_TOPK
    keys = lambda h: slice(h * nk, (h + 1) * nk)
    tops = lambda h: slice(h * topk, (h + 1) * topk)
    cands = lambda h: slice(h * CAND_ROWS, (h + 1) * CAND_ROWS)
    c1_ref[...] = s1_ref[...]
    c2_ref[...] = s2_ref[...]
    rk_ref[...] = jnp.full(rk_ref.shape, float(topk), F32)
    v1_ref[...] = jnp.zeros_like(v1_ref)
    v2_ref[...] = jnp.zeros_like(v2_ref)
    rows16 = lax.broadcasted_iota(jnp.int32, (topk, 1), 0)

    def extract(k, _):
        for h in range(nh):
            cur = c1_ref[keys(h), :]
            m = jnp.max(cur, axis=0, keepdims=True)
            c1_ref[keys(h), :] = jnp.where(cur >= m, NEG_INF, cur)
            v1_ref[tops(h), :] = jnp.where(rows16 == k, m, v1_ref[tops(h), :])
            cur = c2_ref[keys(h), :]
            m = jnp.max(cur, axis=0, keepdims=True)
            hit = cur >= m
            c2_ref[keys(h), :] = jnp.where(hit, NEG_INF, cur)
            rk_ref[keys(h), :] = jnp.where(hit, k.astype(F32), rk_ref[keys(h), :])
            v2_ref[tops(h), :] = jnp.where(rows16 == k, m, v2_ref[tops(h), :])
        return 0

    lax.fori_loop(0, topk, extract, 0)

    for h in range(nh):
        cand_ref[cands(h), :] = _pair_candidates(v1_ref[tops(h), :], v2_ref[tops(h), :])

    def knock(k, _):
        for h in range(nh):
            cur = cand_ref[cands(h), :]
            m = jnp.max(cur, axis=0, keepdims=True)
            cand_ref[cands(h), :] = jnp.where(cur >= m, NEG_INF, cur)
            tau_ref[h:h + 1, :] = m
        return 0

    lax.fori_loop(0, topk, knock, 0)

    for h in range(nh):
        v1 = v1_ref[tops(h), :]
        v2 = v2_ref[tops(h), :]
        tau = tau_ref[h:h + 1, :]
        cand = _pair_candidates(v1, v2)
        mx = v1[0:1, :] + v2[0:1, :]
        z = jnp.sum(jnp.where(cand >= tau, jnp.exp(cand - mx), 0.0), axis=0, keepdims=True)
        s1 = s1_ref[keys(h), :]
        nsel = jnp.zeros_like(s1)
        for bpos in range(topk):
            nsel = nsel + jnp.where(s1 + v2[bpos:bpos + 1, :] >= tau, 1.0, 0.0)
        nsel_ref[keys(h), :] = jnp.where(s1 >= v1[topk - 1:topk, :], nsel, 0.0)
        coef_ref[keys(h), :] = jnp.exp(s1 - v1[0:1, :]) / z
        rank2_ref[keys(h), :] = rk_ref[keys(h), :].astype(rank2_ref.dtype)
        e2_ref[keys(h), :] = jnp.exp(s2_ref[keys(h), :] - v2[0:1, :]).astype(e2_ref.dtype)


def _route(s1t, s2t, tt):
    rows, ntok = s1t.shape
    spec = pl.BlockSpec((rows, tt), lambda i: (0, i))
    f32 = jax.ShapeDtypeStruct((rows, ntok), F32)
    big = pltpu.VMEM((rows, tt), F32)
    small = pltpu.VMEM((PEER_HEADS * PEER_TOPK, tt), F32)
    return pl.pallas_call(
        _route_kernel,
        grid=(ntok // tt,),
        in_specs=[spec, spec],
        out_specs=[spec, spec, spec, spec],
        out_shape=[f32, f32, f32, f32],
        scratch_shapes=[big, big, big, small, small,
                        pltpu.VMEM((PEER_HEADS * CAND_ROWS, tt), F32), pltpu.VMEM((PEER_HEADS, tt), F32)],
        compiler_params=_params(("arbitrary",)),
        name="peer_route",
    )(s1t, s2t)


BF16_ROWS = 16


def _peer_step(chunk, xn_ref, u_ref, vt_ref, rank2_ref, e2_ref, nsel_ref, coef_ref,
               act_next_ref, act_ref, gate_ref, acc_ref, *, te, tt):
    act_next_ref[...] = _dot_nt(u_ref[...], xn_ref[...])
    nsub = te // PEER_NKEYS
    base = chunk * BF16_ROWS
    i1s = [pl.ds(pl.multiple_of(h * (PEER_NKEYS // nsub) * BF16_ROWS + base, BF16_ROWS), BF16_ROWS)
           for h in range(PEER_HEADS)]
    zero = jnp.zeros((PEER_NKEYS, LANES), BF16)
    for lb in range(tt // LANES):
        ls = slice(lb * LANES, (lb + 1) * LANES)
        ns = [nsel_ref[i1s[h], ls] for h in range(PEER_HEADS)]
        cf = [coef_ref[i1s[h], ls] for h in range(PEER_HEADS)]
        for r in range(nsub):
            rs = slice(r * PEER_NKEYS, (r + 1) * PEER_NKEYS)
            w = zero
            for h in range(PEER_HEADS):
                hs = slice(h * PEER_NKEYS, (h + 1) * PEER_NKEYS)
                sel = rank2_ref[hs, ls] < ns[h][r:r + 1, :]
                w = w + jnp.where(sel, e2_ref[hs, ls], zero) * cf[h][r:r + 1, :]
            gate_ref[rs, ls] = (w.astype(F32) * _gelu_tanh(act_ref[rs, ls])).astype(BF16)
    acc_ref[...] += _dot(vt_ref[...], gate_ref[...])
    act_ref[...] = act_next_ref[...]


def _peer_kernel(xn_ref, u_ref, vt_ref, rank2_ref, e2_ref, nsel_ref, coef_ref, x1_ref, mod_ref, gf_ref, o_ref,
                 act_ref, act_next_ref, gate_ref, acc_ref, rank2b_ref, e2b_ref, nselb_ref, coefb_ref, *, te, tt):
    nsub = te // PEER_NKEYS
    assert nsub <= BF16_ROWS
    j = pl.program_id(2)
    last = pl.num_programs(2) - 1

    @pl.when(j == 0)
    def _():
        act_ref[...] = _dot_nt(u_ref[...], xn_ref[...])
        acc_ref[...] = jnp.zeros_like(acc_ref)
        rank2b_ref[...] = rank2_ref[...].astype(BF16)
        e2b_ref[...] = e2_ref[...].astype(BF16)
        nselb_ref[...] = jnp.zeros_like(nselb_ref)
        coefb_ref[...] = jnp.zeros_like(coefb_ref)
        for g in range(PEER_HEADS * PEER_NKEYS // nsub):
            src = slice(g * nsub, (g + 1) * nsub)
            dst = slice(g * BF16_ROWS, g * BF16_ROWS + nsub)
            nselb_ref[dst, :] = nsel_ref[src, :].astype(BF16)
            coefb_ref[dst, :] = coef_ref[src, :].astype(BF16)

    @pl.when(j > 0)
    def _():
        _peer_step(j - 1, xn_ref, u_ref, vt_ref, rank2b_ref, e2b_ref, nselb_ref, coefb_ref,
                   act_next_ref, act_ref, gate_ref, acc_ref, te=te, tt=tt)

    @pl.when(j == last)
    def _():
        x2 = x1_ref[...] + mod_ref[0:1, :] * acc_ref[...].T
        ms = jnp.mean(x2 * x2, axis=-1, keepdims=True)
        o_ref[...] = x2 * lax.rsqrt(ms + NORM_EPS) * gf_ref[...]


def _peer(xn, u, vt, rank2, e2t, nsel, coef, x1, mod, gf, tt, te):
    b, l, d = x1.shape
    nt = l // tt
    ne = PEER_EXPERTS // te
    rows = PEER_HEADS * PEER_NKEYS
    tok = pl.BlockSpec((None, tt, d), lambda bi, i, j: (bi, i, 0))
    tokT = pl.BlockSpec((rows, tt), lambda bi, i, j: (0, bi * nt + i))
    return pl.pallas_call(
        functools.partial(_peer_kernel, te=te, tt=tt),
        grid=(b, nt, ne + 1),
        in_specs=[tok,
                  pl.BlockSpec((te, d), lambda bi, i, j: (jnp.minimum(j, ne - 1), 0)),
                  pl.BlockSpec((d, te), lambda bi, i, j: (0, jnp.maximum(j - 1, 0))),
                  tokT, tokT, tokT, tokT, tok,
                  pl.BlockSpec((None, 8, d), lambda bi, i, j: (bi, 0, 0)),
                  pl.BlockSpec((1, d), lambda bi, i, j: (0, 0))],
        out_specs=tok,
        out_shape=jax.ShapeDtypeStruct((b, l, d), F32),
        scratch_shapes=[pltpu.VMEM((te, tt), F32), pltpu.VMEM((te, tt), F32), pltpu.VMEM((te, tt), BF16),
                        pltpu.VMEM((d, tt), F32), pltpu.VMEM((rows, tt), BF16), pltpu.VMEM((rows, tt), BF16),
                        pltpu.VMEM((rows * BF16_ROWS // (te // PEER_NKEYS), tt), BF16),
                        pltpu.VMEM((rows * BF16_ROWS // (te // PEER_NKEYS), tt), BF16)],
        compiler_params=_params(("arbitrary", "arbitrary", "arbitrary")),
        name="peer_dense",
    )(xn, u, vt, rank2, e2t, nsel, coef, x1, mod, gf)


def _rope_tables(l):
    n_freq = ATT_DK // 4
    inv = ROPE_BASE ** (-2.0 * jnp.arange(n_freq, dtype=F32) / (ATT_DK // 2))
    t = jnp.arange(l, dtype=jnp.int32)
    ang_r = (t // GRID_W).astype(F32)[:, None] * inv[None, :]
    ang_c = (t % GRID_W).astype(F32)[:, None] * inv[None, :]
    cos = jnp.concatenate([jnp.cos(ang_r)] * 2 + [jnp.cos(ang_c)] * 2, axis=-1)
    sin = jnp.concatenate([-jnp.sin(ang_r), jnp.sin(ang_r), -jnp.sin(ang_c), jnp.sin(ang_c)], axis=-1)
    return jnp.tile(cos, (1, 2)), jnp.tile(sin, (1, 2))


def _pick(n, pref):
    for t in pref:
        if n % t == 0:
            return t
    return n


def kernel(x, c, ctx, c_ctx, ada_w, ada_b, norm1_g, norm2_g, w_in, lambda_q1, lambda_k1, lambda_q2, lambda_k2, subln_g, w_attn_up, ssm_a_re, ssm_a_im, ssm_log_dt, ssm_b_re, ssm_b_im, ssm_c_re, ssm_c_im, ssm_d, w_glu, b_glu, w_ssm_up, w_out, peer_w_query, peer_sub_k1, peer_sub_k2, peer_u, peer_v, final_norm_g):
    assert ada_w.shape[0] == 1, "single-layer block"
    b, l, d = x.shape
    nc = ctx.shape[1]
    assert d == D_MODEL and b + 1 <= 8
    ch = SSM_CHUNK

    cin = jnp.concatenate([c, c_ctx[None, :], jnp.zeros((7 - b, d), F32)], axis=0)
    mod = _adaln(cin, ada_w[0], ada_b[0][None, :])
    sh1, sc1, g1, sh2, sc2, g2 = [mod[:, k * d:(k + 1) * d] for k in range(6)]
    mod1 = jnp.stack([sh1[:b], sc1[:b]], axis=1)
    mod1c = jnp.stack([sh1[b], sc1[b]], axis=0)
    zpad = jnp.zeros((b, 5, d), F32)
    mod2 = jnp.concatenate([jnp.stack([g1[:b], sh2[:b], sc2[:b]], axis=1), zpad], axis=1)
    mod3 = jnp.concatenate([g2[:b, None, :], jnp.zeros((b, 7, d), F32)], axis=1)

    w = w_in[0]
    n_q = 2 * ATT_HEADS * ATT_DK
    o_k, o_v, o_u, o_g = n_q, 2 * n_q, 2 * n_q + ATT_HEADS * ATT_DV, 2 * n_q + ATT_HEADS * ATT_DV + SSM_W
    wq, wk, wv, wu, wg = [w[:, a:e].astype(BF16) for a, e in
                          ((0, o_k), (o_k, o_v), (o_v, o_u), (o_u, o_g), (o_g, w.shape[1]))]
    cos, sin = _rope_tables(l)
    g1n = norm1_g[0][None, :]

    tm = _pick(l, (512, 256, 128))
    q, k, v, u, sg = _inproj_x(x, mod1, g1n, cos, sin, wq, wk, wv, wu, wg, tm)
    kc, vc, uc = _inproj_c(ctx, mod1c, g1n, wk, wv, wu)

    row = lambda a: a[0][None, :].astype(F32)
    attn = _attention(q, kc, vc, k, v, row(lambda_q1), row(lambda_k1), row(lambda_q2), row(lambda_k2),
                      row(subln_g), _pick(l, (256, 128)), _pick(l, (256, 128)))

    nsteps = max(1, math.ceil(math.log2((l + nc) // ch)))
    sw = _ssm_weights(ssm_a_re[0], ssm_a_im[0], ssm_log_dt[0], ssm_b_re[0], ssm_b_im[0],
                      ssm_c_re[0], ssm_c_im[0], ssm_d[0], nsteps)
    to_groups = lambda a, n: jnp.transpose(a.reshape(b, n // ch, ch, SSM_GROUPS, SSM_GROUP),
                                           (3, 1, 0, 2, 4)).reshape(SSM_GROUPS, (n // ch) * b, ch * SSM_GROUP)
    yg = _ssm(to_groups(u, l), to_groups(uc, nc), sw, b, nsteps)
    y = jnp.transpose(yg.reshape(SSM_GROUPS, l // ch, b, ch, SSM_GROUP), (2, 1, 3, 0, 4)).reshape(b, l, SSM_W)

    half = PEER_DKEY // 2
    zk = jnp.zeros((PEER_NKEYS, half), F32)
    k1p = jnp.concatenate([peer_sub_k1[0], zk], axis=1).astype(BF16)
    k2p = jnp.concatenate([zk, peer_sub_k2[0]], axis=1).astype(BF16)
    x1, xn2, s1t, s2t = _merge(x, attn, y, sg, mod2, norm2_g[0][None, :],
                               w_glu[0].astype(BF16), b_glu[0][None, :], w_attn_up[0].astype(BF16),
                               w_ssm_up[0].astype(BF16), w_out[0].astype(BF16), peer_w_query[0].astype(BF16),
                               k1p, k2p, tm)

    nsel, coef, rank2, e2t = _route(s1t, s2t, _pick(b * l, (256, 128)))
    tt = _pick(l, (512, 256, 128))
    out = _peer(xn2, peer_u[0].astype(BF16), peer_v[0].T.astype(BF16), rank2, e2t, nsel, coef, x1, mod3,
                final_norm_g[None, :], tt, 1024)
    return out
```

```python
import functools
import math

import jax
import jax.numpy as jnp
from jax import lax
from jax.experimental import pallas as pl
from jax.experimental.pallas import tpu as pltpu

F32 = jnp.float32
BF16 = jnp.bfloat16
HIGHEST = lax.Precision.HIGHEST

LANES = 128
MXU_WIDTH = 256
VMEM_LIMIT = 56 * 1024 * 1024

D_MODEL = 1024
NORM_EPS = 1e-6
GRID_W = 64
ATT_HEADS = 8
ATT_DK = 64
ATT_DV = 2 * ATT_DK
ATT_Q_SCALE = ATT_DK ** -0.5 * math.log2(math.e)
ROPE_BASE = 10000.0
SSM_GROUP = 16
SSM_W = D_MODEL // 2
SSM_GROUPS = SSM_W // SSM_GROUP
SSM_STATE = 64
SSM_CHUNK = 16
PEER_HEADS = 8
PEER_NKEYS = 128
PEER_EXPERTS = PEER_NKEYS * PEER_NKEYS
PEER_TOPK = 16
PEER_DKEY = 128
LAMBDA_INIT = 0.8 - 0.6 * math.exp(-0.3 * 0)

NEG_INF = float("-inf")


def _dot(a, b, precision=None):
    return jnp.dot(a, b, preferred_element_type=F32, precision=precision)


def _dot_nt(a, b):
    return lax.dot_general(a, b, (((1,), (1,)), ((), ())), preferred_element_type=F32)


def _sigmoid(z):
    return 1.0 / (1.0 + jnp.exp(-z))


def _gelu_tanh(z):
    return 0.5 * z * (1.0 + jnp.tanh(math.sqrt(2.0 / math.pi) * (z + 0.044715 * (z * z * z))))


def _norm_mod(xf, g, shift, scale):
    ms = jnp.mean(xf * xf, axis=-1, keepdims=True)
    xn = xf * lax.rsqrt(ms + NORM_EPS) * g
    return xn * (1.0 + scale) + shift


def _params(sem):
    return pltpu.CompilerParams(dimension_semantics=sem, vmem_limit_bytes=VMEM_LIMIT)


def _const_spec(shape):
    nd = len(shape)
    return pl.BlockSpec(shape, lambda *_: (0,) * nd, pipeline_mode=pl.Buffered(1))


def _adaln_kernel(c_ref, w_ref, b_ref, o_ref):
    a = c_ref[...]
    act = a * _sigmoid(a)
    o_ref[...] = _dot(act, w_ref[...], precision=HIGHEST) + b_ref[...]


def _adaln(cin, w, b):
    n = w.shape[1]
    tn = 1536
    return pl.pallas_call(
        _adaln_kernel,
        grid=(n // tn,),
        in_specs=[pl.BlockSpec((8, D_MODEL), lambda j: (0, 0)),
                  pl.BlockSpec((D_MODEL, tn), lambda j: (0, j)),
                  pl.BlockSpec((1, tn), lambda j: (0, j))],
        out_specs=pl.BlockSpec((8, tn), lambda j: (0, j)),
        out_shape=jax.ShapeDtypeStruct((8, n), F32),
        compiler_params=_params(("arbitrary",)),
        name="adaln",
    )(cin, w, b)


def _rope_store(h, cos, sin, low, scale, o_ref):
    for hh in range(ATT_HEADS):
        blk = h[:, hh * LANES:(hh + 1) * LANES]
        sw = jnp.where(low, pltpu.roll(blk, LANES - 16, 1), pltpu.roll(blk, 16, 1))
        o_ref[:, hh * LANES:(hh + 1) * LANES] = ((blk * cos + sw * sin) * scale).astype(o_ref.dtype)


def _inproj_x_kernel(x_ref, mod_ref, g_ref, cos_ref, sin_ref, wq_ref, wk_ref, wv_ref, wu_ref, wg_ref,
                     q_ref, k_ref, v_ref, u_ref, sg_ref):
    xb = _norm_mod(x_ref[...], g_ref[...], mod_ref[0:1, :], mod_ref[1:2, :]).astype(BF16)
    cos = cos_ref[...]
    sin = sin_ref[...]
    lane = lax.broadcasted_iota(jnp.int32, (1, LANES), 1)
    low = (lane & 31) < 16
    _rope_store(_dot(xb, wq_ref[...]), cos, sin, low, ATT_Q_SCALE, q_ref)
    _rope_store(_dot(xb, wk_ref[...]), cos, sin, low, 1.0, k_ref)
    v_ref[...] = _dot(xb, wv_ref[...]).astype(v_ref.dtype)
    u_ref[...] = _dot(xb, wu_ref[...])
    sg_ref[...] = _sigmoid(_dot(xb, wg_ref[...])).astype(sg_ref.dtype)


def _inproj_x(x, mod, g, cos, sin, wq, wk, wv, wu, wg, tm):
    b, l, d = x.shape
    tok = lambda n: pl.BlockSpec((None, tm, n), lambda bi, i: (bi, i, 0))
    return pl.pallas_call(
        _inproj_x_kernel,
        grid=(b, l // tm),
        in_specs=[tok(d),
                  pl.BlockSpec((None, 2, d), lambda bi, i: (bi, 0, 0)),
                  pl.BlockSpec((1, d), lambda bi, i: (0, 0)),
                  pl.BlockSpec((tm, LANES), lambda bi, i: (i, 0)),
                  pl.BlockSpec((tm, LANES), lambda bi, i: (i, 0)),
                  _const_spec(wq.shape), _const_spec(wk.shape), _const_spec(wv.shape),
                  _const_spec(wu.shape), _const_spec(wg.shape)],
        out_specs=[tok(d), tok(d), tok(d), tok(SSM_W), tok(2 * d)],
        out_shape=[jax.ShapeDtypeStruct((b, l, d), BF16),
                   jax.ShapeDtypeStruct((b, l, d), BF16),
                   jax.ShapeDtypeStruct((b, l, d), BF16),
                   jax.ShapeDtypeStruct((b, l, SSM_W), F32),
                   jax.ShapeDtypeStruct((b, l, 2 * d), BF16)],
        compiler_params=_params(("arbitrary", "arbitrary")),
        name="inproj_x",
    )(x, mod, g, cos, sin, wq, wk, wv, wu, wg)


def _inproj_c_kernel(x_ref, mod_ref, g_ref, wk_ref, wv_ref, wu_ref, k_ref, v_ref, u_ref):
    xb = _norm_mod(x_ref[...], g_ref[...], mod_ref[0:1, :], mod_ref[1:2, :]).astype(BF16)
    k_ref[...] = _dot(xb, wk_ref[...]).astype(k_ref.dtype)
    v_ref[...] = _dot(xb, wv_ref[...]).astype(v_ref.dtype)
    u_ref[...] = _dot(xb, wu_ref[...])


def _inproj_c(ctx, mod, g, wk, wv, wu):
    b, n, d = ctx.shape
    tok = lambda w: pl.BlockSpec((None, n, w), lambda bi: (bi, 0, 0))
    return pl.pallas_call(
        _inproj_c_kernel,
        grid=(b,),
        in_specs=[tok(d),
                  pl.BlockSpec((2, d), lambda bi: (0, 0)),
                  pl.BlockSpec((1, d), lambda bi: (0, 0)),
                  _const_spec(wk.shape), _const_spec(wv.shape), _const_spec(wu.shape)],
        out_specs=[tok(d), tok(d), tok(SSM_W)],
        out_shape=[jax.ShapeDtypeStruct((b, n, d), BF16),
                   jax.ShapeDtypeStruct((b, n, d), BF16),
                   jax.ShapeDtypeStruct((b, n, SSM_W), F32)],
        compiler_params=_params(("arbitrary",)),
        name="inproj_ctx",
    )(ctx, mod, g, wk, wv, wu)


def _attn_chunks(kc_ref, kx_ref, tk):
    nc, nx = kc_ref.shape[0], kx_ref.shape[0]
    return [(False, 0, nc, 0)] + [(True, j * tk, tk, nc + j * tk) for j in range(nx // tk)]


def _attn_step(q_ref, kc_ref, vc_ref, kx_ref, vx_ref, lq1_ref, lk1_ref, lq2_ref, lk2_ref, sg_ref, o_ref,
               s_new_ref, m_new_ref, s_old_ref, m_old_ref, *, tq, tk, score, finish):
    if score:
        q = q_ref[...]
        lane = lax.broadcasted_iota(jnp.int32, (1, LANES), 1)
        zero = jnp.zeros_like(q)
        qs = jnp.concatenate([jnp.where(lane < ATT_DK, q, zero), jnp.where(lane >= ATT_DK, q, zero)], axis=0)
        m_lane = jnp.full((2 * tq, LANES), NEG_INF, F32)
    if finish:
        m = jnp.broadcast_to(jnp.max(m_old_ref[...], axis=1, keepdims=True), (2 * tq, LANES))
        acc = jnp.zeros((2 * tq, 2 * ATT_DV), F32)
    for latent, off, n, col in _attn_chunks(kc_ref, kx_ref, tk):
        k_ref, v_ref = (kx_ref, vx_ref) if latent else (kc_ref, vc_ref)
        if score:
            s = _dot_nt(qs, k_ref[off:off + n, :])
            s_new_ref[:, col:col + n] = s
            smax = s[:, 0:LANES]
            for g in range(1, n // LANES):
                smax = jnp.maximum(smax, s[:, g * LANES:(g + 1) * LANES])
            m_lane = jnp.maximum(m_lane, smax)
        if finish:
            mc = m
            if score:
                mc = m + jnp.minimum(jnp.abs(smax), 0.0)
            ps = [jnp.exp2(s_old_ref[:, col + g * LANES:col + (g + 1) * LANES] - mc).astype(BF16)
                  for g in range(n // LANES)]
            v1 = jnp.concatenate([v_ref[off:off + n, :], jnp.ones((n, ATT_DV), BF16)], axis=1)
            acc = acc + _dot(jnp.concatenate(ps, axis=1), v1)
    if score:
        m_new_ref[...] = m_lane
    if finish:
        o = acc[:, :ATT_DV] / acc[:, ATT_DV:]
        lam = (jnp.exp(jnp.sum(lq1_ref[...] * lk1_ref[...], axis=1, keepdims=True))
               - jnp.exp(jnp.sum(lq2_ref[...] * lk2_ref[...], axis=1, keepdims=True)) + LAMBDA_INIT)
        o = o[:tq] - lam * o[tq:]
        ms = jnp.mean(o * o, axis=-1, keepdims=True)
        o = o * lax.rsqrt(ms + NORM_EPS) * sg_ref[...] * (1.0 - LAMBDA_INIT)
        o_ref[...] = o.astype(o_ref.dtype)


def _attn_kernel(q_ref, kc_ref, vc_ref, kx_ref, vx_ref, lq1_ref, lk1_ref, lq2_ref, lk2_ref, sg_ref, o_ref,
                 s0_ref, s1_ref, m0_ref, m1_ref, *, tq, tk):
    i = pl.program_id(2)
    step = functools.partial(_attn_step, q_ref, kc_ref, vc_ref, kx_ref, vx_ref, lq1_ref, lk1_ref, lq2_ref, lk2_ref,
                             sg_ref, o_ref, tq=tq, tk=tk)

    @pl.when(i == 0)
    def _():
        step(s0_ref, m0_ref, s1_ref, m1_ref, score=True, finish=False)

    @pl.when(i % 2 == 1)
    def _():
        step(s1_ref, m1_ref, s0_ref, m0_ref, score=True, finish=True)

    @pl.when(jnp.logical_and(i > 0, i % 2 == 0))
    def _():
        step(s0_ref, m0_ref, s1_ref, m1_ref, score=True, finish=True)


def _attention(q, kc, vc, kx, vx, lq1, lk1, lq2, lk2, subln_g, tq, tk):
    b, l, d = q.shape
    nc = kc.shape[1]
    nq = l // tq
    vec = lambda n: pl.BlockSpec((1, n), lambda bi, h, i: (0, 0))
    return pl.pallas_call(
        functools.partial(_attn_kernel, tq=tq, tk=tk),
        grid=(b, ATT_HEADS, nq + 1),
        in_specs=[pl.BlockSpec((None, tq, LANES), lambda bi, h, i: (bi, jnp.minimum(i, nq - 1), h)),
                  pl.BlockSpec((None, nc, LANES), lambda bi, h, i: (bi, 0, h)),
                  pl.BlockSpec((None, nc, LANES), lambda bi, h, i: (bi, 0, h)),
                  pl.BlockSpec((None, l, LANES), lambda bi, h, i: (bi, 0, h)),
                  pl.BlockSpec((None, l, LANES), lambda bi, h, i: (bi, 0, h)),
                  vec(ATT_DK), vec(ATT_DK), vec(ATT_DK), vec(ATT_DK), vec(ATT_DV)],
        out_specs=pl.BlockSpec((None, tq, LANES), lambda bi, h, i: (bi, jnp.maximum(i - 1, 0), h)),
        out_shape=jax.ShapeDtypeStruct((b, l, d), BF16),
        scratch_shapes=[pltpu.VMEM((2 * tq, nc + l), F32), pltpu.VMEM((2 * tq, nc + l), F32),
                        pltpu.VMEM((2 * tq, LANES), F32), pltpu.VMEM((2 * tq, LANES), F32)],
        compiler_params=_params(("arbitrary", "arbitrary", "arbitrary")),
        name="diff_attn",
    )(q, kc, vc, kx, vx, lq1, lk1, lq2, lk2, subln_g)


def _ssm_weights(a_re, a_im, log_dt, b_re, b_im, c_re, c_im, d_skip, nsteps):
    ch, g, n, p = SSM_CHUNK, SSM_GROUPS, SSM_STATE, SSM_GROUP
    ar = a_re.astype(F32)
    ai = a_im.astype(F32)
    dt = jnp.exp(log_dt.astype(F32))[..., None]
    lam = ar * dt
    theta = ai * dt
    mag = jnp.exp(lam)
    abar_re = mag * jnp.cos(theta)
    abar_im = mag * jnp.sin(theta)
    den = ar * ar + ai * ai
    nr = abar_re - 1.0
    ni = abar_im
    fr = (nr * ar + ni * ai) / den
    fi = (ni * ar - nr * ai) / den
    br = b_re.astype(F32)
    bi = b_im.astype(F32)
    bb_re = fr[..., None] * br - fi[..., None] * bi
    bb_im = fr[..., None] * bi + fi[..., None] * br
    cr = c_re.astype(F32)
    ci = c_im.astype(F32)

    j = jnp.arange(ch + 1, dtype=F32)[:, None, None, None]
    pr = jnp.exp(lam[None] * j) * jnp.cos(theta[None] * j)
    pi = jnp.exp(lam[None] * j) * jnp.sin(theta[None] * j)

    cb_re = cr[..., None] * bb_re[:, :, None] - ci[..., None] * bb_im[:, :, None]
    cb_im = cr[..., None] * bb_im[:, :, None] + ci[..., None] * bb_re[:, :, None]
    kker = (jnp.einsum("dgpnq,ldgn->dlgpq", cb_re, pr[:ch], precision=HIGHEST)
            - jnp.einsum("dgpnq,ldgn->dlgpq", cb_im, pi[:ch], precision=HIGHEST))
    ii = jnp.arange(ch)[:, None]
    jj = jnp.arange(ch)[None, :]
    kf = jnp.where((jj >= ii)[..., None, None, None], kker[0][jnp.clip(jj - ii, 0, ch - 1)], 0.0)
    kb = jnp.where((ii >= jj)[..., None, None, None], kker[1][jnp.clip(ii - jj, 0, ch - 1)], 0.0)
    mm = jnp.transpose(kf + kb, (2, 0, 4, 1, 3)).reshape(g, ch * p, ch * p)
    dvec = jnp.tile(d_skip.astype(F32).reshape(g, 1, p), (1, ch, 1)).reshape(g, ch * p)
    mm = mm + jnp.eye(ch * p, dtype=F32)[None] * dvec[:, None, :]

    def bpow(d, idx):
        pre = pr[idx, d][:, :, :, None]
        pim = pi[idx, d][:, :, :, None]
        re = pre * bb_re[d][None] - pim * bb_im[d][None]
        im = pre * bb_im[d][None] + pim * bb_re[d][None]
        both = jnp.stack([re, im], axis=0)
        return jnp.transpose(both, (2, 1, 4, 0, 3)).reshape(g, ch * p, 2 * n)

    def cpow(d, idx):
        pre = pr[idx, d][:, :, None, :]
        pim = pi[idx, d][:, :, None, :]
        ca_re = cr[d][None] * pre - ci[d][None] * pim
        ca_im = cr[d][None] * pim + ci[d][None] * pre
        both = jnp.stack([ca_re, -ca_im], axis=0)
        return jnp.transpose(both, (2, 0, 4, 1, 3)).reshape(g, 2 * n, ch * p)

    ar_idx = jnp.arange(ch)
    bpf = bpow(0, ch - 1 - ar_idx)
    bpb = bpow(1, ar_idx)
    cpf = cpow(0, ar_idx + 1)
    cpb = cpow(1, ch - ar_idx)

    re, im = pr[ch], pi[ch]
    rows1, rows2 = [], []
    for _ in range(nsteps):
        rows1.append(jnp.concatenate([re, re], axis=-1))
        rows2.append(jnp.concatenate([-im, im], axis=-1))
        re, im = re * re - im * im, 2.0 * re * im
    pad = [jnp.zeros_like(rows1[0])] * (16 - nsteps)
    coef = jnp.stack([jnp.stack(rows1 + pad, axis=2), jnp.stack(rows2 + pad, axis=2)], axis=2)
    coef = jnp.transpose(coef, (1, 0, 2, 3, 4)).reshape(g, 4, 16, 2 * n)
    return mm, bpf, bpb, cpf, cpb, coef


def _ssm_kernel(ux_ref, uc_ref, m_ref, bpf_ref, bpb_ref, cpf_ref, cpb_ref, coef_ref, y_ref, *, nb, nsteps):
    ux = ux_ref[...]
    uc = uc_ref[...]
    rx, rc = ux.shape[0], uc.shape[0]
    r = rx + rc
    bpf = bpf_ref[...]
    bpb = bpb_ref[...]
    hf = jnp.concatenate([_dot(uc, bpf, HIGHEST), _dot(ux, bpf, HIGHEST)], axis=0)
    gb = jnp.concatenate([_dot(ux, bpb, HIGHEST), _dot(uc, bpb, HIGHEST)], axis=0)
    row = lax.broadcasted_iota(jnp.int32, (r, 1), 0)
    half = SSM_STATE
    for k in range(nsteps):
        s = nb << k
        if s >= r:
            break
        sh = jnp.where(row >= s, pltpu.roll(hf, s, 0), 0.0)
        hf = hf + coef_ref[0, k:k + 1, :] * sh + coef_ref[1, k:k + 1, :] * pltpu.roll(sh, half, 1)
        sh = jnp.where(row < r - s, pltpu.roll(gb, r - s, 0), 0.0)
        gb = gb + coef_ref[2, k:k + 1, :] * sh + coef_ref[3, k:k + 1, :] * pltpu.roll(sh, half, 1)
    hprev = jnp.where(row >= nb, pltpu.roll(hf, nb, 0), 0.0)[rc:]
    gnext = jnp.where(row < r - nb, pltpu.roll(gb, r - nb, 0), 0.0)[:rx]
    y_ref[...] = (_dot(ux, m_ref[...], HIGHEST) + _dot(hprev, cpf_ref[...], HIGHEST)
                  + _dot(gnext, cpb_ref[...], HIGHEST))


def _ssm(ux, uc, weights, nb, nsteps):
    g, rx, w = ux.shape
    rc = uc.shape[1]
    mm, bpf, bpb, cpf, cpb, coef = weights
    per_g = lambda a: pl.BlockSpec((None,) + a.shape[1:], lambda gi: (gi,) + (0,) * (a.ndim - 1))
    return pl.pallas_call(
        functools.partial(_ssm_kernel, nb=nb, nsteps=nsteps),
        grid=(g,),
        in_specs=[per_g(ux), per_g(uc), per_g(mm), per_g(bpf), per_g(bpb), per_g(cpf), per_g(cpb), per_g(coef)],
        out_specs=pl.BlockSpec((None, rx, w), lambda gi: (gi, 0, 0)),
        out_shape=jax.ShapeDtypeStruct((g, rx, w), F32),
        compiler_params=_params(("arbitrary",)),
        name="s5_chunked",
    )(ux, uc, mm, bpf, bpb, cpf, cpb, coef)


def _merge_kernel(x_ref, attn_ref, y_ref, sg_ref, mod_ref, g2_ref, wglu_ref, bglu_ref, wau_ref, wsu_ref, wout_ref,
                  wqry_ref, k1_ref, k2_ref, x1_ref, xn_ref, s1_ref, s2_ref):
    d = D_MODEL
    yg = _gelu_tanh(y_ref[...])
    ysg = yg * _sigmoid(_dot(yg.astype(BF16), wglu_ref[...]) + bglu_ref[...])
    a_up = _dot(attn_ref[...], wau_ref[...])
    s_up = _dot(ysg.astype(BF16), wsu_ref[...])
    mix_in = sg_ref[:, :d].astype(F32) * a_up + sg_ref[:, d:].astype(F32) * s_up
    mix = _dot(mix_in.astype(BF16), wout_ref[...])
    x1 = x_ref[...] + mod_ref[0:1, :] * mix
    x1_ref[...] = x1
    xn = _norm_mod(x1, g2_ref[...], mod_ref[1:2, :], mod_ref[2:3, :]).astype(BF16)
    xn_ref[...] = xn
    qb = _dot(xn, wqry_ref[...]).astype(BF16)
    for h in range(PEER_HEADS):
        qh = qb[:, h * LANES:(h + 1) * LANES]
        s1_ref[h * LANES:(h + 1) * LANES, :] = _dot_nt(k1_ref[...], qh)
        s2_ref[h * LANES:(h + 1) * LANES, :] = _dot_nt(k2_ref[...], qh)


def _merge(x, attn, y, sg, mod, g2, wglu, bglu, wau, wsu, wout, wqry, k1p, k2p, tm):
    b, l, d = x.shape
    ntok = b * l
    nt = l // tm
    tok = lambda n: pl.BlockSpec((None, tm, n), lambda bi, i: (bi, i, 0))
    tokT = pl.BlockSpec((PEER_HEADS * PEER_NKEYS, tm), lambda bi, i: (0, bi * nt + i))
    return pl.pallas_call(
        _merge_kernel,
        grid=(b, nt),
        in_specs=[tok(d), tok(d), tok(SSM_W), tok(2 * d),
                  pl.BlockSpec((None, 8, d), lambda bi, i: (bi, 0, 0)),
                  pl.BlockSpec((1, d), lambda bi, i: (0, 0)),
                  _const_spec(wglu.shape), _const_spec(bglu.shape), _const_spec(wau.shape),
                  _const_spec(wsu.shape), _const_spec(wout.shape), _const_spec(wqry.shape),
                  _const_spec(k1p.shape), _const_spec(k2p.shape)],
        out_specs=[tok(d), tok(d), tokT, tokT],
        out_shape=[jax.ShapeDtypeStruct((b, l, d), F32),
                   jax.ShapeDtypeStruct((b, l, d), BF16),
                   jax.ShapeDtypeStruct((PEER_HEADS * PEER_NKEYS, ntok), F32),
                   jax.ShapeDtypeStruct((PEER_HEADS * PEER_NKEYS, ntok), F32)],
        compiler_params=_params(("arbitrary", "arbitrary")),
        name="merge_peer_query",
    )(x, attn, y, sg, mod, g2, wglu, bglu, wau, wsu, wout, wqry, k1p, k2p)


CAND_ROWS = 80


def _pair_candidates(v1, v2):
    rows8 = lax.broadcasted_iota(jnp.int32, (8, 1), 0)
    blocks = [v1[0:1, :] + v2]
    for a in range(1, 8):
        blocks.append(jnp.where(rows8 < PEER_TOPK // (a + 1), v1[a:a + 1, :] + v2[0:8, :], NEG_INF))
    blocks.append(v1[8:16, :] + v2[0:1, :])
    return jnp.concatenate(blocks, axis=0)


def _route_kernel(s1_ref, s2_ref, nsel_ref, coef_ref, rank2_ref, e2_ref,
                  c1_ref, c2_ref, rk_ref, v1_ref, v2_ref, cand_ref, tau_ref):
    nh, nk, topk = PEER_HEADS, PEER_NKEYS, PEER_TOPK
    keys = lambda h: slice(h * nk, (h + 1) * nk)
    tops = lambda h: slice(h * topk, (h + 1) * topk)
    cands = lambda h: slice(h * CAND_ROWS, (h + 1) * CAND_ROWS)
    c1_ref[...] = s1_ref[...]
    c2_ref[...] = s2_ref[...]
    rk_ref[...] = jnp.full(rk_ref.shape, float(topk), F32)
    v1_ref[...] = jnp.zeros_like(v1_ref)
    v2_ref[...] = jnp.zeros_like(v2_ref)
    rows16 = lax.broadcasted_iota(jnp.int32, (topk, 1), 0)

    def extract(k, _):
        for h in range(nh):
            cur = c1_ref[keys(h), :]
            m = jnp.max(cur, axis=0, keepdims=True)
            c1_ref[keys(h), :] = jnp.where(cur >= m, NEG_INF, cur)
            v1_ref[tops(h), :] = jnp.where(rows16 == k, m, v1_ref[tops(h), :])
            cur = c2_ref[keys(h), :]
            m = jnp.max(cur, axis=0, keepdims=True)
            hit = cur >= m
            c2_ref[keys(h), :] = jnp.where(hit, NEG_INF, cur)
            rk_ref[keys(h), :] = jnp.where(hit, k.astype(F32), rk_ref[keys(h), :])
            v2_ref[tops(h), :] = jnp.where(rows16 == k, m, v2_ref[tops(h), :])
        return 0

    lax.fori_loop(0, topk, extract, 0)

    for h in range(nh):
        cand_ref[cands(h), :] = _pair_candidates(v1_ref[tops(h), :], v2_ref[tops(h), :])

    def knock(k, _):
        for h in range(nh):
            cur = cand_ref[cands(h), :]
            m = jnp.max(cur, axis=0, keepdims=True)
            cand_ref[cands(h), :] = jnp.where(cur >= m, NEG_INF, cur)
            tau_ref[h:h + 1, :] = m
        return 0

    lax.fori_loop(0, topk, knock, 0)

    for h in range(nh):
        v1 = v1_ref[tops(h), :]
        v2 = v2_ref[tops(h), :]
        tau = tau_ref[h:h + 1, :]
        cand = _pair_candidates(v1, v2)
        mx = v1[0:1, :] + v2[0:1, :]
        z = jnp.sum(jnp.where(cand >= tau, jnp.exp(cand - mx), 0.0), axis=0, keepdims=True)
        s1 = s1_ref[keys(h), :]
        nsel = jnp.zeros_like(s1)
        for bpos in range(topk):
            nsel = nsel + jnp.where(s1 + v2[bpos:bpos + 1, :] >= tau, 1.0, 0.0)
        nsel_ref[keys(h), :] = jnp.where(s1 >= v1[topk - 1:topk, :], nsel, 0.0)
        coef_ref[keys(h), :] = jnp.exp(s1 - v1[0:1, :]) / z
        rank2_ref[keys(h), :] = rk_ref[keys(h), :].astype(rank2_ref.dtype)
        e2_ref[keys(h), :] = jnp.exp(s2_ref[keys(h), :] - v2[0:1, :]).astype(e2_ref.dtype)


def _route(s1t, s2t, tt):
    rows, ntok = s1t.shape
    spec = pl.BlockSpec((rows, tt), lambda i: (0, i))
    f32 = jax.ShapeDtypeStruct((rows, ntok), F32)
    bf16 = jax.ShapeDtypeStruct((rows, ntok), BF16)
    big = pltpu.VMEM((rows, tt), F32)
    small = pltpu.VMEM((PEER_HEADS * PEER_TOPK, tt), F32)
    return pl.pallas_call(
        _route_kernel,
        grid=(ntok // tt,),
        in_specs=[spec, spec],
        out_specs=[spec, spec, spec, spec],
        out_shape=[f32, f32, bf16, bf16],
        scratch_shapes=[big, big, big, small, small,
                        pltpu.VMEM((PEER_HEADS * CAND_ROWS, tt), F32), pltpu.VMEM((PEER_HEADS, tt), F32)],
        compiler_params=_params(("arbitrary",)),
        name="peer_route",
    )(s1t, s2t)


def _peer_step(xn_ref, u_ref, vt_ref, rank2_ref, e2_ref, nsel_ref, coef_ref, act_next_ref, act_ref, gate_ref, acc_ref,
               *, te, tt):
    act_next_ref[...] = _dot_nt(u_ref[...], xn_ref[...]).astype(BF16)
    nsub = te // PEER_NKEYS
    zero = jnp.zeros((PEER_NKEYS, LANES), BF16)
    for lb in range(tt // LANES):
        ls = slice(lb * LANES, (lb + 1) * LANES)
        ns = [nsel_ref[h, :, ls].astype(BF16) for h in range(PEER_HEADS)]
        cf = [coef_ref[h, :, ls].astype(BF16) for h in range(PEER_HEADS)]
        for r in range(nsub):
            rs = slice(r * PEER_NKEYS, (r + 1) * PEER_NKEYS)
            w = zero
            for h in range(PEER_HEADS):
                hs = slice(h * PEER_NKEYS, (h + 1) * PEER_NKEYS)
                sel = rank2_ref[hs, ls] < ns[h][r:r + 1, :]
                w = w + jnp.where(sel, e2_ref[hs, ls], zero) * cf[h][r:r + 1, :]
            gate_ref[rs, ls] = (w.astype(F32) * _gelu_tanh(act_ref[rs, ls].astype(F32))).astype(BF16)
    acc_ref[...] += _dot(vt_ref[...], gate_ref[...])
    act_ref[...] = act_next_ref[...]


def _peer_kernel(xn_ref, u_ref, vt_ref, rank2_ref, e2_ref, nsel_ref, coef_ref, o_ref,
                 act_ref, act_next_ref, gate_ref, rank2b_ref, e2b_ref, *, te, tt):
    j = pl.program_id(2)

    @pl.when(j == 0)
    def _():
        act_ref[...] = _dot_nt(u_ref[...], xn_ref[...]).astype(BF16)
        o_ref[...] = jnp.zeros_like(o_ref)
        rank2b_ref[...] = rank2_ref[...]
        e2b_ref[...] = e2_ref[...]

    @pl.when(j > 0)
    def _():
        _peer_step(xn_ref, u_ref, vt_ref, rank2b_ref, e2b_ref, nsel_ref, coef_ref,
                   act_next_ref, act_ref, gate_ref, o_ref, te=te, tt=tt)


def _peer(xn, u, vt, rank2, e2t, nsel, coef, tt, te):
    b, l, d = xn.shape
    nt = l // tt
    ne = PEER_EXPERTS // te
    nsub = te // PEER_NKEYS
    rows = PEER_HEADS * PEER_NKEYS
    ntok = b * l
    once = dict(pipeline_mode=pl.Buffered(1))
    tokT = pl.BlockSpec((rows, tt), lambda bi, i, j: (0, bi * nt + i), **once)
    chunk_rows = pl.BlockSpec((PEER_HEADS, None, nsub, tt), lambda bi, i, j: (0, jnp.maximum(j - 1, 0), 0, bi * nt + i))
    split = lambda a: a.reshape(PEER_HEADS, PEER_NKEYS // nsub, nsub, ntok)
    return pl.pallas_call(
        functools.partial(_peer_kernel, te=te, tt=tt),
        grid=(b, nt, ne + 1),
        in_specs=[pl.BlockSpec((None, tt, d), lambda bi, i, j: (bi, i, 0), **once),
                  pl.BlockSpec((te, d), lambda bi, i, j: (jnp.minimum(j, ne - 1), 0)),
                  pl.BlockSpec((d, te), lambda bi, i, j: (0, jnp.maximum(j - 1, 0))),
                  tokT, tokT, chunk_rows, chunk_rows],
        out_specs=pl.BlockSpec((d, tt), lambda bi, i, j: (0, bi * nt + i)),
        out_shape=jax.ShapeDtypeStruct((d, ntok), F32),
        scratch_shapes=[pltpu.VMEM((te, tt), BF16), pltpu.VMEM((te, tt), BF16), pltpu.VMEM((te, tt), BF16),
                        pltpu.VMEM((rows, tt), BF16), pltpu.VMEM((rows, tt), BF16)],
        compiler_params=_params(("arbitrary", "arbitrary", "arbitrary")),
        name="peer_dense",
    )(xn, u, vt, rank2, e2t, split(nsel), split(coef))


def _final_kernel(x1_ref, pt_ref, mod_ref, gf_ref, o_ref):
    x2 = x1_ref[...] + mod_ref[0:1, :] * pt_ref[...].T
    ms = jnp.mean(x2 * x2, axis=-1, keepdims=True)
    o_ref[...] = x2 * lax.rsqrt(ms + NORM_EPS) * gf_ref[...]


def _final(x1, peer_t, mod, gf, tm):
    b, l, d = x1.shape
    nt = l // tm
    tok = pl.BlockSpec((None, tm, d), lambda bi, i: (bi, i, 0))
    return pl.pallas_call(
        _final_kernel,
        grid=(b, nt),
        in_specs=[tok,
                  pl.BlockSpec((d, tm), lambda bi, i: (0, bi * nt + i)),
                  pl.BlockSpec((None, 8, d), lambda bi, i: (bi, 0, 0)),
                  pl.BlockSpec((1, d), lambda bi, i: (0, 0))],
        out_specs=tok,
        out_shape=jax.ShapeDtypeStruct((b, l, d), F32),
        compiler_params=_params(("arbitrary", "arbitrary")),
        name="final_norm",
    )(x1, peer_t, mod, gf)


def _rope_tables(l):
    n_freq = ATT_DK // 4
    inv = ROPE_BASE ** (-2.0 * jnp.arange(n_freq, dtype=F32) / (ATT_DK // 2))
    t = jnp.arange(l, dtype=jnp.int32)
    ang_r = (t // GRID_W).astype(F32)[:, None] * inv[None, :]
    ang_c = (t % GRID_W).astype(F32)[:, None] * inv[None, :]
    cos = jnp.concatenate([jnp.cos(ang_r)] * 2 + [jnp.cos(ang_c)] * 2, axis=-1)
    sin = jnp.concatenate([-jnp.sin(ang_r), jnp.sin(ang_r), -jnp.sin(ang_c), jnp.sin(ang_c)], axis=-1)
    return jnp.tile(cos, (1, 2)), jnp.tile(sin, (1, 2))


def _pick(n, pref):
    for t in pref:
        if n % t == 0:
            return t
    return n


def kernel(x, c, ctx, c_ctx, ada_w, ada_b, norm1_g, norm2_g, w_in, lambda_q1, lambda_k1, lambda_q2, lambda_k2, subln_g, w_attn_up, ssm_a_re, ssm_a_im, ssm_log_dt, ssm_b_re, ssm_b_im, ssm_c_re, ssm_c_im, ssm_d, w_glu, b_glu, w_ssm_up, w_out, peer_w_query, peer_sub_k1, peer_sub_k2, peer_u, peer_v, final_norm_g):
    assert ada_w.shape[0] == 1, "single-layer block"
    b, l, d = x.shape
    nc = ctx.shape[1]
    assert d == D_MODEL and b + 1 <= 8
    ch = SSM_CHUNK

    cin = jnp.concatenate([c, c_ctx[None, :], jnp.zeros((7 - b, d), F32)], axis=0)
    mod = _adaln(cin, ada_w[0], ada_b[0][None, :])
    sh1, sc1, g1, sh2, sc2, g2 = [mod[:, k * d:(k + 1) * d] for k in range(6)]
    mod1 = jnp.stack([sh1[:b], sc1[:b]], axis=1)
    mod1c = jnp.stack([sh1[b], sc1[b]], axis=0)
    zpad = jnp.zeros((b, 5, d), F32)
    mod2 = jnp.concatenate([jnp.stack([g1[:b], sh2[:b], sc2[:b]], axis=1), zpad], axis=1)
    mod3 = jnp.concatenate([g2[:b, None, :], jnp.zeros((b, 7, d), F32)], axis=1)

    w = w_in[0]
    n_q = 2 * ATT_HEADS * ATT_DK
    o_k, o_v, o_u, o_g = n_q, 2 * n_q, 2 * n_q + ATT_HEADS * ATT_DV, 2 * n_q + ATT_HEADS * ATT_DV + SSM_W
    wq, wk, wv, wu, wg = [w[:, a:e].astype(BF16) for a, e in
                          ((0, o_k), (o_k, o_v), (o_v, o_u), (o_u, o_g), (o_g, w.shape[1]))]
    cos, sin = _rope_tables(l)
    g1n = norm1_g[0][None, :]

    tm = _pick(l, (512, 256, 128))
    q, k, v, u, sg = _inproj_x(x, mod1, g1n, cos, sin, wq, wk, wv, wu, wg, tm)
    kc, vc, uc = _inproj_c(ctx, mod1c, g1n, wk, wv, wu)

    row = lambda a: a[0][None, :].astype(F32)
    attn = _attention(q, kc, vc, k, v, row(lambda_q1), row(lambda_k1), row(lambda_q2), row(lambda_k2),
                      row(subln_g), _pick(l, (256, 128)), _pick(l, (256, 128)))

    nsteps = max(1, math.ceil(math.log2((l + nc) // ch)))
    sw = _ssm_weights(ssm_a_re[0], ssm_a_im[0], ssm_log_dt[0], ssm_b_re[0], ssm_b_im[0],
                      ssm_c_re[0], ssm_c_im[0], ssm_d[0], nsteps)
    to_groups = lambda a, n: jnp.transpose(a.reshape(b, n // ch, ch, SSM_GROUPS, SSM_GROUP),
                                           (3, 1, 0, 2, 4)).reshape(SSM_GROUPS, (n // ch) * b, ch * SSM_GROUP)
    yg = _ssm(to_groups(u, l), to_groups(uc, nc), sw, b, nsteps)
    y = jnp.transpose(yg.reshape(SSM_GROUPS, l // ch, b, ch, SSM_GROUP), (2, 1, 3, 0, 4)).reshape(b, l, SSM_W)

    half = PEER_DKEY // 2
    zk = jnp.zeros((PEER_NKEYS, half), F32)
    k1p = jnp.concatenate([peer_sub_k1[0], zk], axis=1).astype(BF16)
    k2p = jnp.concatenate([zk, peer_sub_k2[0]], axis=1).astype(BF16)
    x1, xn2, s1t, s2t = _merge(x, attn, y, sg, mod2, norm2_g[0][None, :],
                               w_glu[0].astype(BF16), b_glu[0][None, :], w_attn_up[0].astype(BF16),
                               w_ssm_up[0].astype(BF16), w_out[0].astype(BF16), peer_w_query[0].astype(BF16),
                               k1p, k2p, tm)

    nsel, coef, rank2, e2t = _route(s1t, s2t, _pick(b * l, (256, 128)))
    peer_t = _peer(xn2, peer_u[0].astype(BF16), peer_v[0].T.astype(BF16), rank2, e2t, nsel, coef,
                   _pick(l, (1024, 512, 256, 128)), 1024)
    return _final(x1, peer_t, mod3, final_norm_g[None, :], tm)
```

```python
import functools
import math

import jax
import jax.numpy as jnp
from jax import lax
from jax.experimental import pallas as pl
from jax.experimental.pallas import tpu as pltpu

F32 = jnp.float32
BF16 = jnp.bfloat16
HIGHEST = lax.Precision.HIGHEST

LANES = 128
MXU_WIDTH = 256
MXU_ROWS = 512
VMEM_LIMIT = 56 * 1024 * 1024

D_MODEL = 1024
NORM_EPS = 1e-6
GRID_W = 64
ATT_HEADS = 8
ATT_DK = 64
ATT_DV = 2 * ATT_DK
ATT_Q_SCALE = ATT_DK ** -0.5 * math.log2(math.e)
ROPE_BASE = 10000.0
SSM_GROUP = 16
SSM_W = D_MODEL // 2
SSM_GROUPS = SSM_W // SSM_GROUP
SSM_STATE = 64
SSM_CHUNK = 16
PEER_HEADS = 8
PEER_NKEYS = 128
PEER_EXPERTS = PEER_NKEYS * PEER_NKEYS
PEER_TOPK = 16
PEER_DKEY = 128
LAMBDA_INIT = 0.8 - 0.6 * math.exp(-0.3 * 0)

NEG_INF = float("-inf")


def _dot(a, b, precision=None):
    return jnp.dot(a, b, preferred_element_type=F32, precision=precision)


def _dot_nt(a, b):
    return lax.dot_general(a, b, (((1,), (1,)), ((), ())), preferred_element_type=F32)


def _sigmoid(z):
    return 1.0 / (1.0 + jnp.exp(-z))


def _gelu_tanh(z):
    return 0.5 * z * (1.0 + jnp.tanh(math.sqrt(2.0 / math.pi) * (z + 0.044715 * (z * z * z))))


def _norm_mod(xf, g, shift, scale):
    ms = jnp.mean(xf * xf, axis=-1, keepdims=True)
    xn = xf * lax.rsqrt(ms + NORM_EPS) * g
    return xn * (1.0 + scale) + shift


def _params(sem):
    return pltpu.CompilerParams(dimension_semantics=sem, vmem_limit_bytes=VMEM_LIMIT)


def _const_spec(shape):
    nd = len(shape)
    return pl.BlockSpec(shape, lambda *_: (0,) * nd, pipeline_mode=pl.Buffered(1))


def _adaln_kernel(c_ref, w_ref, b_ref, o_ref):
    a = c_ref[...]
    act = a * _sigmoid(a)
    o_ref[...] = _dot(act, w_ref[...], precision=HIGHEST) + b_ref[...]


def _adaln(cin, w, b):
    n = w.shape[1]
    tn = 1536
    return pl.pallas_call(
        _adaln_kernel,
        grid=(n // tn,),
        in_specs=[pl.BlockSpec((8, D_MODEL), lambda j: (0, 0)),
                  pl.BlockSpec((D_MODEL, tn), lambda j: (0, j)),
                  pl.BlockSpec((1, tn), lambda j: (0, j))],
        out_specs=pl.BlockSpec((8, tn), lambda j: (0, j)),
        out_shape=jax.ShapeDtypeStruct((8, n), F32),
        compiler_params=_params(("arbitrary",)),
        name="adaln",
    )(cin, w, b)


def _rope_store(h, cos, sin, low, scale, o_ref):
    for hh in range(ATT_HEADS):
        blk = h[:, hh * LANES:(hh + 1) * LANES]
        sw = jnp.where(low, pltpu.roll(blk, LANES - 16, 1), pltpu.roll(blk, 16, 1))
        o_ref[:, hh * LANES:(hh + 1) * LANES] = ((blk * cos + sw * sin) * scale).astype(o_ref.dtype)


def _inproj_x_kernel(x_ref, mod_ref, g_ref, cos_ref, sin_ref, wq_ref, wk_ref, wv_ref, wu_ref, wg_ref,
                     q_ref, k_ref, v_ref, u_ref, sg_ref):
    xb = _norm_mod(x_ref[...], g_ref[...], mod_ref[0:1, :], mod_ref[1:2, :]).astype(BF16)
    cos = cos_ref[...]
    sin = sin_ref[...]
    lane = lax.broadcasted_iota(jnp.int32, (1, LANES), 1)
    low = (lane & 31) < 16
    _rope_store(_dot(xb, wq_ref[...]), cos, sin, low, ATT_Q_SCALE, q_ref)
    _rope_store(_dot(xb, wk_ref[...]), cos, sin, low, 1.0, k_ref)
    v_ref[...] = _dot(xb, wv_ref[...]).astype(v_ref.dtype)
    u_ref[...] = _dot(xb, wu_ref[...])
    sg_ref[...] = _sigmoid(_dot(xb, wg_ref[...])).astype(sg_ref.dtype)


def _inproj_x(x, mod, g, cos, sin, wq, wk, wv, wu, wg, tm):
    b, l, d = x.shape
    tok = lambda n: pl.BlockSpec((None, tm, n), lambda bi, i: (bi, i, 0))
    return pl.pallas_call(
        _inproj_x_kernel,
        grid=(b, l // tm),
        in_specs=[tok(d),
                  pl.BlockSpec((None, 2, d), lambda bi, i: (bi, 0, 0)),
                  pl.BlockSpec((1, d), lambda bi, i: (0, 0)),
                  pl.BlockSpec((tm, LANES), lambda bi, i: (i, 0)),
                  pl.BlockSpec((tm, LANES), lambda bi, i: (i, 0)),
                  _const_spec(wq.shape), _const_spec(wk.shape), _const_spec(wv.shape),
                  _const_spec(wu.shape), _const_spec(wg.shape)],
        out_specs=[tok(d), tok(d), tok(d), tok(SSM_W), tok(2 * d)],
        out_shape=[jax.ShapeDtypeStruct((b, l, d), BF16),
                   jax.ShapeDtypeStruct((b, l, d), BF16),
                   jax.ShapeDtypeStruct((b, l, d), BF16),
                   jax.ShapeDtypeStruct((b, l, SSM_W), F32),
                   jax.ShapeDtypeStruct((b, l, 2 * d), BF16)],
        compiler_params=_params(("arbitrary", "arbitrary")),
        name="inproj_x",
    )(x, mod, g, cos, sin, wq, wk, wv, wu, wg)


def _inproj_c_kernel(x_ref, mod_ref, g_ref, wk_ref, wv_ref, wu_ref, k_ref, v_ref, u_ref):
    xb = _norm_mod(x_ref[...], g_ref[...], mod_ref[0:1, :], mod_ref[1:2, :]).astype(BF16)
    k_ref[...] = _dot(xb, wk_ref[...]).astype(k_ref.dtype)
    v_ref[...] = _dot(xb, wv_ref[...]).astype(v_ref.dtype)
    u_ref[...] = _dot(xb, wu_ref[...])


def _inproj_c(ctx, mod, g, wk, wv, wu):
    b, n, d = ctx.shape
    tok = lambda w: pl.BlockSpec((None, n, w), lambda bi: (bi, 0, 0))
    return pl.pallas_call(
        _inproj_c_kernel,
        grid=(b,),
        in_specs=[tok(d),
                  pl.BlockSpec((2, d), lambda bi: (0, 0)),
                  pl.BlockSpec((1, d), lambda bi: (0, 0)),
                  _const_spec(wk.shape), _const_spec(wv.shape), _const_spec(wu.shape)],
        out_specs=[tok(d), tok(d), tok(SSM_W)],
        out_shape=[jax.ShapeDtypeStruct((b, n, d), BF16),
                   jax.ShapeDtypeStruct((b, n, d), BF16),
                   jax.ShapeDtypeStruct((b, n, SSM_W), F32)],
        compiler_params=_params(("arbitrary",)),
        name="inproj_ctx",
    )(ctx, mod, g, wk, wv, wu)


def _attn_chunks(kc_ref, kx_ref, tk):
    nc, nx = kc_ref.shape[0], kx_ref.shape[0]
    return [(False, 0, nc, 0)] + [(True, j * tk, tk, nc + j * tk) for j in range(nx // tk)]


def _attn_step(q_ref, kc_ref, vc_ref, kx_ref, vx_ref, lq1_ref, lk1_ref, lq2_ref, lk2_ref, sg_ref, o_ref,
               s_new_ref, m_new_ref, s_old_ref, m_old_ref, *, tq, tk, score, finish):
    if score:
        q = q_ref[...]
        lane = lax.broadcasted_iota(jnp.int32, (1, LANES), 1)
        zero = jnp.zeros_like(q)
        qs = jnp.concatenate([jnp.where(lane < ATT_DK, q, zero), jnp.where(lane >= ATT_DK, q, zero)], axis=0)
        m_lane = jnp.full((2 * tq, LANES), NEG_INF, F32)
    if finish:
        m = jnp.broadcast_to(jnp.max(m_old_ref[...], axis=1, keepdims=True), (2 * tq, LANES))
        acc = jnp.zeros((2 * tq, 2 * ATT_DV), F32)
    for latent, off, n, col in _attn_chunks(kc_ref, kx_ref, tk):
        k_ref, v_ref = (kx_ref, vx_ref) if latent else (kc_ref, vc_ref)
        if score:
            s = _dot_nt(qs, k_ref[off:off + n, :])
            s_new_ref[:, col:col + n] = s
            smax = s[:, 0:LANES]
            for g in range(1, n // LANES):
                smax = jnp.maximum(smax, s[:, g * LANES:(g + 1) * LANES])
            m_lane = jnp.maximum(m_lane, smax)
        if finish:
            mc = m
            if score:
                mc = m + jnp.minimum(jnp.abs(smax), 0.0)
            ps = [jnp.exp2(s_old_ref[:, col + g * LANES:col + (g + 1) * LANES] - mc).astype(BF16)
                  for g in range(n // LANES)]
            v1 = jnp.concatenate([v_ref[off:off + n, :], jnp.ones((n, ATT_DV), BF16)], axis=1)
            acc = acc + _dot(jnp.concatenate(ps, axis=1), v1)
    if score:
        m_new_ref[...] = m_lane
    if finish:
        o = acc[:, :ATT_DV] / acc[:, ATT_DV:]
        lam = (jnp.exp(jnp.sum(lq1_ref[...] * lk1_ref[...], axis=1, keepdims=True))
               - jnp.exp(jnp.sum(lq2_ref[...] * lk2_ref[...], axis=1, keepdims=True)) + LAMBDA_INIT)
        o = o[:tq] - lam * o[tq:]
        ms = jnp.mean(o * o, axis=-1, keepdims=True)
        o = o * lax.rsqrt(ms + NORM_EPS) * sg_ref[...] * (1.0 - LAMBDA_INIT)
        o_ref[...] = o.astype(o_ref.dtype)


def _attn_kernel(q_ref, kc_ref, vc_ref, kx_ref, vx_ref, lq1_ref, lk1_ref, lq2_ref, lk2_ref, sg_ref, o_ref,
                 s0_ref, s1_ref, m0_ref, m1_ref, *, tq, tk):
    i = pl.program_id(2)
    step = functools.partial(_attn_step, q_ref, kc_ref, vc_ref, kx_ref, vx_ref, lq1_ref, lk1_ref, lq2_ref, lk2_ref,
                             sg_ref, o_ref, tq=tq, tk=tk)

    @pl.when(i == 0)
    def _():
        step(s0_ref, m0_ref, s1_ref, m1_ref, score=True, finish=False)

    @pl.when(i % 2 == 1)
    def _():
        step(s1_ref, m1_ref, s0_ref, m0_ref, score=True, finish=True)

    @pl.when(jnp.logical_and(i > 0, i % 2 == 0))
    def _():
        step(s0_ref, m0_ref, s1_ref, m1_ref, score=True, finish=True)


def _attention(q, kc, vc, kx, vx, lq1, lk1, lq2, lk2, subln_g, tq, tk):
    b, l, d = q.shape
    nc = kc.shape[1]
    nq = l // tq
    vec = lambda n: pl.BlockSpec((1, n), lambda bi, h, i: (0, 0))
    return pl.pallas_call(
        functools.partial(_attn_kernel, tq=tq, tk=tk),
        grid=(b, ATT_HEADS, nq + 1),
        in_specs=[pl.BlockSpec((None, tq, LANES), lambda bi, h, i: (bi, jnp.minimum(i, nq - 1), h)),
                  pl.BlockSpec((None, nc, LANES), lambda bi, h, i: (bi, 0, h)),
                  pl.BlockSpec((None, nc, LANES), lambda bi, h, i: (bi, 0, h)),
                  pl.BlockSpec((None, l, LANES), lambda bi, h, i: (bi, 0, h)),
                  pl.BlockSpec((None, l, LANES), lambda bi, h, i: (bi, 0, h)),
                  vec(ATT_DK), vec(ATT_DK), vec(ATT_DK), vec(ATT_DK), vec(ATT_DV)],
        out_specs=pl.BlockSpec((None, tq, LANES), lambda bi, h, i: (bi, jnp.maximum(i - 1, 0), h)),
        out_shape=jax.ShapeDtypeStruct((b, l, d), BF16),
        scratch_shapes=[pltpu.VMEM((2 * tq, nc + l), F32), pltpu.VMEM((2 * tq, nc + l), F32),
                        pltpu.VMEM((2 * tq, LANES), F32), pltpu.VMEM((2 * tq, LANES), F32)],
        compiler_params=_params(("arbitrary", "arbitrary", "arbitrary")),
        name="diff_attn",
    )(q, kc, vc, kx, vx, lq1, lk1, lq2, lk2, subln_g)


def _ssm_weights(a_re, a_im, log_dt, b_re, b_im, c_re, c_im, d_skip, nsteps):
    ch, g, n, p = SSM_CHUNK, SSM_GROUPS, SSM_STATE, SSM_GROUP
    ar = a_re.astype(F32)
    ai = a_im.astype(F32)
    dt = jnp.exp(log_dt.astype(F32))[..., None]
    lam = ar * dt
    theta = ai * dt
    mag = jnp.exp(lam)
    abar_re = mag * jnp.cos(theta)
    abar_im = mag * jnp.sin(theta)
    den = ar * ar + ai * ai
    nr = abar_re - 1.0
    ni = abar_im
    fr = (nr * ar + ni * ai) / den
    fi = (ni * ar - nr * ai) / den
    br = b_re.astype(F32)
    bi = b_im.astype(F32)
    bb_re = fr[..., None] * br - fi[..., None] * bi
    bb_im = fr[..., None] * bi + fi[..., None] * br
    cr = c_re.astype(F32)
    ci = c_im.astype(F32)

    j = jnp.arange(ch + 1, dtype=F32)[:, None, None, None]
    pr = jnp.exp(lam[None] * j) * jnp.cos(theta[None] * j)
    pi = jnp.exp(lam[None] * j) * jnp.sin(theta[None] * j)

    cb_re = cr[..., None] * bb_re[:, :, None] - ci[..., None] * bb_im[:, :, None]
    cb_im = cr[..., None] * bb_im[:, :, None] + ci[..., None] * bb_re[:, :, None]
    kker = (jnp.einsum("dgpnq,ldgn->dlgpq", cb_re, pr[:ch], precision=HIGHEST)
            - jnp.einsum("dgpnq,ldgn->dlgpq", cb_im, pi[:ch], precision=HIGHEST))
    ii = jnp.arange(ch)[:, None]
    jj = jnp.arange(ch)[None, :]
    kf = jnp.where((jj >= ii)[..., None, None, None], kker[0][jnp.clip(jj - ii, 0, ch - 1)], 0.0)
    kb = jnp.where((ii >= jj)[..., None, None, None], kker[1][jnp.clip(ii - jj, 0, ch - 1)], 0.0)
    mm = jnp.transpose(kf + kb, (2, 0, 4, 1, 3)).reshape(g, ch * p, ch * p)
    dvec = jnp.tile(d_skip.astype(F32).reshape(g, 1, p), (1, ch, 1)).reshape(g, ch * p)
    mm = mm + jnp.eye(ch * p, dtype=F32)[None] * dvec[:, None, :]

    def bpow(d, idx):
        pre = pr[idx, d][:, :, :, None]
        pim = pi[idx, d][:, :, :, None]
        re = pre * bb_re[d][None] - pim * bb_im[d][None]
        im = pre * bb_im[d][None] + pim * bb_re[d][None]
        both = jnp.stack([re, im], axis=0)
        return jnp.transpose(both, (2, 1, 4, 0, 3)).reshape(g, ch * p, 2 * n)

    def cpow(d, idx):
        pre = pr[idx, d][:, :, None, :]
        pim = pi[idx, d][:, :, None, :]
        ca_re = cr[d][None] * pre - ci[d][None] * pim
        ca_im = cr[d][None] * pim + ci[d][None] * pre
        both = jnp.stack([ca_re, -ca_im], axis=0)
        return jnp.transpose(both, (2, 0, 4, 1, 3)).reshape(g, 2 * n, ch * p)

    ar_idx = jnp.arange(ch)
    bpf = bpow(0, ch - 1 - ar_idx)
    bpb = bpow(1, ar_idx)
    cpf = cpow(0, ar_idx + 1)
    cpb = cpow(1, ch - ar_idx)

    re, im = pr[ch], pi[ch]
    rows1, rows2 = [], []
    for _ in range(nsteps):
        rows1.append(jnp.concatenate([re, re], axis=-1))
        rows2.append(jnp.concatenate([-im, im], axis=-1))
        re, im = re * re - im * im, 2.0 * re * im
    pad = [jnp.zeros_like(rows1[0])] * (16 - nsteps)
    coef = jnp.stack([jnp.stack(rows1 + pad, axis=2), jnp.stack(rows2 + pad, axis=2)], axis=2)
    coef = jnp.transpose(coef, (1, 0, 2, 3, 4)).reshape(g, 4, 16, 2 * n)
    return mm, bpf, bpb, cpf, cpb, coef


def _ssm_kernel(ux_ref, uc_ref, m_ref, bpf_ref, bpb_ref, cpf_ref, cpb_ref, coef_ref, y_ref, *, nb, nsteps):
    ux = ux_ref[...]
    uc = uc_ref[...]
    rx, rc = ux.shape[0], uc.shape[0]
    r = rx + rc
    bpf = bpf_ref[...]
    bpb = bpb_ref[...]
    hf = jnp.concatenate([_dot(uc, bpf, HIGHEST), _dot(ux, bpf, HIGHEST)], axis=0)
    gb = jnp.concatenate([_dot(ux, bpb, HIGHEST), _dot(uc, bpb, HIGHEST)], axis=0)
    row = lax.broadcasted_iota(jnp.int32, (r, 1), 0)
    half = SSM_STATE
    for k in range(nsteps):
        s = nb << k
        if s >= r:
            break
        sh = jnp.where(row >= s, pltpu.roll(hf, s, 0), 0.0)
        hf = hf + coef_ref[0, k:k + 1, :] * sh + coef_ref[1, k:k + 1, :] * pltpu.roll(sh, half, 1)
        sh = jnp.where(row < r - s, pltpu.roll(gb, r - s, 0), 0.0)
        gb = gb + coef_ref[2, k:k + 1, :] * sh + coef_ref[3, k:k + 1, :] * pltpu.roll(sh, half, 1)
    hprev = jnp.where(row >= nb, pltpu.roll(hf, nb, 0), 0.0)[rc:]
    gnext = jnp.where(row < r - nb, pltpu.roll(gb, r - nb, 0), 0.0)[:rx]
    y_ref[...] = (_dot(ux, m_ref[...], HIGHEST) + _dot(hprev, cpf_ref[...], HIGHEST)
                  + _dot(gnext, cpb_ref[...], HIGHEST))


def _ssm(ux, uc, weights, nb, nsteps):
    g, rx, w = ux.shape
    rc = uc.shape[1]
    mm, bpf, bpb, cpf, cpb, coef = weights
    per_g = lambda a: pl.BlockSpec((None,) + a.shape[1:], lambda gi: (gi,) + (0,) * (a.ndim - 1))
    return pl.pallas_call(
        functools.partial(_ssm_kernel, nb=nb, nsteps=nsteps),
        grid=(g,),
        in_specs=[per_g(ux), per_g(uc), per_g(mm), per_g(bpf), per_g(bpb), per_g(cpf), per_g(cpb), per_g(coef)],
        out_specs=pl.BlockSpec((None, rx, w), lambda gi: (gi, 0, 0)),
        out_shape=jax.ShapeDtypeStruct((g, rx, w), F32),
        compiler_params=_params(("arbitrary",)),
        name="s5_chunked",
    )(ux, uc, mm, bpf, bpb, cpf, cpb, coef)


def _merge_kernel(x_ref, attn_ref, y_ref, sg_ref, mod_ref, g2_ref, wglu_ref, bglu_ref, wau_ref, wsu_ref, wout_ref,
                  wqry_ref, k1_ref, k2_ref, x1_ref, xn_ref, s1_ref, s2_ref):
    d = D_MODEL
    yg = _gelu_tanh(y_ref[...])
    ysg = yg * _sigmoid(_dot(yg.astype(BF16), wglu_ref[...]) + bglu_ref[...])
    a_up = _dot(attn_ref[...], wau_ref[...])
    s_up = _dot(ysg.astype(BF16), wsu_ref[...])
    mix_in = sg_ref[:, :d].astype(F32) * a_up + sg_ref[:, d:].astype(F32) * s_up
    mix = _dot(mix_in.astype(BF16), wout_ref[...])
    x1 = x_ref[...] + mod_ref[0:1, :] * mix
    x1_ref[...] = x1
    xn = _norm_mod(x1, g2_ref[...], mod_ref[1:2, :], mod_ref[2:3, :]).astype(BF16)
    xn_ref[...] = xn
    qb = _dot(xn, wqry_ref[...]).astype(BF16)
    for h in range(PEER_HEADS):
        qh = qb[:, h * LANES:(h + 1) * LANES]
        s1_ref[h * LANES:(h + 1) * LANES, :] = _dot_nt(k1_ref[...], qh)
        s2_ref[h * LANES:(h + 1) * LANES, :] = _dot_nt(k2_ref[...], qh)


def _merge(x, attn, y, sg, mod, g2, wglu, bglu, wau, wsu, wout, wqry, k1p, k2p, tm):
    b, l, d = x.shape
    ntok = b * l
    nt = l // tm
    tok = lambda n: pl.BlockSpec((None, tm, n), lambda bi, i: (bi, i, 0))
    tokT = pl.BlockSpec((PEER_HEADS * PEER_NKEYS, tm), lambda bi, i: (0, bi * nt + i))
    return pl.pallas_call(
        _merge_kernel,
        grid=(b, nt),
        in_specs=[tok(d), tok(d), tok(SSM_W), tok(2 * d),
                  pl.BlockSpec((None, 8, d), lambda bi, i: (bi, 0, 0)),
                  pl.BlockSpec((1, d), lambda bi, i: (0, 0)),
                  _const_spec(wglu.shape), _const_spec(bglu.shape), _const_spec(wau.shape),
                  _const_spec(wsu.shape), _const_spec(wout.shape), _const_spec(wqry.shape),
                  _const_spec(k1p.shape), _const_spec(k2p.shape)],
        out_specs=[tok(d), tok(d), tokT, tokT],
        out_shape=[jax.ShapeDtypeStruct((b, l, d), F32),
                   jax.ShapeDtypeStruct((b, l, d), BF16),
                   jax.ShapeDtypeStruct((PEER_HEADS * PEER_NKEYS, ntok), F32),
                   jax.ShapeDtypeStruct((PEER_HEADS * PEER_NKEYS, ntok), F32)],
        compiler_params=_params(("arbitrary", "arbitrary")),
        name="merge_peer_query",
    )(x, attn, y, sg, mod, g2, wglu, bglu, wau, wsu, wout, wqry, k1p, k2p)


CAND_ROWS = 80


def _pair_candidates(v1, v2):
    rows8 = lax.broadcasted_iota(jnp.int32, (8, 1), 0)
    blocks = [v1[0:1, :] + v2]
    for a in range(1, 8):
        blocks.append(jnp.where(rows8 < PEER_TOPK // (a + 1), v1[a:a + 1, :] + v2[0:8, :], NEG_INF))
    blocks.append(v1[8:16, :] + v2[0:1, :])
    return jnp.concatenate(blocks, axis=0)


def _route_kernel(s1_ref, s2_ref, nsel_ref, coef_ref, rank2_ref, e2_ref,
                  c1_ref, c2_ref, rk_ref, v1_ref, v2_ref, cand_ref, tau_ref):
    nh, nk, topk = PEER_HEADS, PEER_NKEYS, PEER_TOPK
    keys = lambda h: slice(h * nk, (h + 1) * nk)
    tops = lambda h: slice(h * topk, (h + 1) * topk)
    cands = lambda h: slice(h * CAND_ROWS, (h + 1) * CAND_ROWS)
    c1_ref[...] = s1_ref[...]
    c2_ref[...] = s2_ref[...]
    rk_ref[...] = jnp.full(rk_ref.shape, float(topk), F32)
    v1_ref[...] = jnp.zeros_like(v1_ref)
    v2_ref[...] = jnp.zeros_like(v2_ref)
    rows16 = lax.broadcasted_iota(jnp.int32, (topk, 1), 0)

    def extract(k, _):
        for h in range(nh):
            cur = c1_ref[keys(h), :]
            m = jnp.max(cur, axis=0, keepdims=True)
            c1_ref[keys(h), :] = jnp.where(cur >= m, NEG_INF, cur)
            v1_ref[tops(h), :] = jnp.where(rows16 == k, m, v1_ref[tops(h), :])
            cur = c2_ref[keys(h), :]
            m = jnp.max(cur, axis=0, keepdims=True)
            hit = cur >= m
            c2_ref[keys(h), :] = jnp.where(hit, NEG_INF, cur)
            rk_ref[keys(h), :] = jnp.where(hit, k.astype(F32), rk_ref[keys(h), :])
            v2_ref[tops(h), :] = jnp.where(rows16 == k, m, v2_ref[tops(h), :])
        return 0

    lax.fori_loop(0, topk, extract, 0)

    for h in range(nh):
        cand_ref[cands(h), :] = _pair_candidates(v1_ref[tops(h), :], v2_ref[tops(h), :])

    def knock(k, _):
        for h in range(nh):
            cur = cand_ref[cands(h), :]
            m = jnp.max(cur, axis=0, keepdims=True)
            cand_ref[cands(h), :] = jnp.where(cur >= m, NEG_INF, cur)
            tau_ref[h:h + 1, :] = m
        return 0

    lax.fori_loop(0, topk, knock, 0)

    for h in range(nh):
        v1 = v1_ref[tops(h), :]
        v2 = v2_ref[tops(h), :]
        tau = tau_ref[h:h + 1, :]
        cand = _pair_candidates(v1, v2)
        mx = v1[0:1, :] + v2[0:1, :]
        z = jnp.sum(jnp.where(cand >= tau, jnp.exp(cand - mx), 0.0), axis=0, keepdims=True)
        s1 = s1_ref[keys(h), :]
        nsel = jnp.zeros_like(s1)
        for bpos in range(topk):
            nsel = nsel + jnp.where(s1 + v2[bpos:bpos + 1, :] >= tau, 1.0, 0.0)
        nsel_ref[keys(h), :] = jnp.where(s1 >= v1[topk - 1:topk, :], nsel, 0.0)
        coef_ref[keys(h), :] = jnp.exp(s1 - v1[0:1, :]) / z
        rank2_ref[keys(h), :] = rk_ref[keys(h), :].astype(rank2_ref.dtype)
        e2_ref[keys(h), :] = jnp.exp(s2_ref[keys(h), :] - v2[0:1, :]).astype(e2_ref.dtype)


def _route(s1t, s2t, tt):
    rows, ntok = s1t.shape
    spec = pl.BlockSpec((rows, tt), lambda i: (0, i))
    f32 = jax.ShapeDtypeStruct((rows, ntok), F32)
    bf16 = jax.ShapeDtypeStruct((rows, ntok), BF16)
    big = pltpu.VMEM((rows, tt), F32)
    small = pltpu.VMEM((PEER_HEADS * PEER_TOPK, tt), F32)
    return pl.pallas_call(
        _route_kernel,
        grid=(ntok // tt,),
        in_specs=[spec, spec],
        out_specs=[spec, spec, spec, spec],
        out_shape=[f32, f32, bf16, bf16],
        scratch_shapes=[big, big, big, small, small,
                        pltpu.VMEM((PEER_HEADS * CAND_ROWS, tt), F32), pltpu.VMEM((PEER_HEADS, tt), F32)],
        compiler_params=_params(("arbitrary",)),
        name="peer_route",
    )(s1t, s2t)


def _peer_step(xn_ref, u_ref, vt_ref, rank2_ref, e2_ref, nsel_ref, coef_ref, act_next_ref, act_ref, gate_ref, acc_ref,
               *, te, tt):
    for hf in range(te // MXU_ROWS):
        rows = slice(hf * MXU_ROWS, (hf + 1) * MXU_ROWS)
        act_next_ref[rows, :] = _dot_nt(u_ref[rows, :], xn_ref[...]).astype(BF16)
    nsub = te // PEER_NKEYS
    zero = jnp.zeros((PEER_NKEYS, LANES), BF16)
    for lb in range(tt // LANES):
        ls = slice(lb * LANES, (lb + 1) * LANES)
        ns = [nsel_ref[h, :, ls].astype(BF16) for h in range(PEER_HEADS)]
        cf = [coef_ref[h, :, ls].astype(BF16) for h in range(PEER_HEADS)]
        for r in range(nsub):
            rs = slice(r * PEER_NKEYS, (r + 1) * PEER_NKEYS)
            w = zero
            for h in range(PEER_HEADS):
                hs = slice(h * PEER_NKEYS, (h + 1) * PEER_NKEYS)
                sel = rank2_ref[hs, ls] < ns[h][r:r + 1, :]
                w = w + jnp.where(sel, e2_ref[hs, ls], zero) * cf[h][r:r + 1, :]
            gate_ref[rs, ls] = (w.astype(F32) * _gelu_tanh(act_ref[rs, ls].astype(F32))).astype(BF16)
    for hf in range(D_MODEL // MXU_ROWS):
        rows = slice(hf * MXU_ROWS, (hf + 1) * MXU_ROWS)
        acc_ref[rows, :] += _dot(vt_ref[rows, :], gate_ref[...])
    act_ref[...] = act_next_ref[...]


def _peer_kernel(xn_ref, u_ref, vt_ref, rank2_ref, e2_ref, nsel_ref, coef_ref, o_ref,
                 act_ref, act_next_ref, gate_ref, rank2b_ref, e2b_ref, *, te, tt):
    j = pl.program_id(2)

    @pl.when(j == 0)
    def _():
        act_ref[...] = _dot_nt(u_ref[...], xn_ref[...]).astype(BF16)
        o_ref[...] = jnp.zeros_like(o_ref)
        rank2b_ref[...] = rank2_ref[...]
        e2b_ref[...] = e2_ref[...]

    @pl.when(j > 0)
    def _():
        _peer_step(xn_ref, u_ref, vt_ref, rank2b_ref, e2b_ref, nsel_ref, coef_ref,
                   act_next_ref, act_ref, gate_ref, o_ref, te=te, tt=tt)


def _peer(xn, u, vt, rank2, e2t, nsel, coef, tt, te):
    b, l, d = xn.shape
    nt = l // tt
    ne = PEER_EXPERTS // te
    nsub = te // PEER_NKEYS
    rows = PEER_HEADS * PEER_NKEYS
    ntok = b * l
    once = dict(pipeline_mode=pl.Buffered(1))
    tokT = pl.BlockSpec((rows, tt), lambda bi, i, j: (0, bi * nt + i), **once)
    chunk_rows = pl.BlockSpec((PEER_HEADS, None, nsub, tt), lambda bi, i, j: (0, jnp.maximum(j - 1, 0), 0, bi * nt + i))
    split = lambda a: a.reshape(PEER_HEADS, PEER_NKEYS // nsub, nsub, ntok)
    return pl.pallas_call(
        functools.partial(_peer_kernel, te=te, tt=tt),
        grid=(b, nt, ne + 1),
        in_specs=[pl.BlockSpec((None, tt, d), lambda bi, i, j: (bi, i, 0), **once),
                  pl.BlockSpec((te, d), lambda bi, i, j: (jnp.minimum(j, ne - 1), 0)),
                  pl.BlockSpec((d, te), lambda bi, i, j: (0, jnp.maximum(j - 1, 0))),
                  tokT, tokT, chunk_rows, chunk_rows],
        out_specs=pl.BlockSpec((d, tt), lambda bi, i, j: (0, bi * nt + i)),
        out_shape=jax.ShapeDtypeStruct((d, ntok), F32),
        scratch_shapes=[pltpu.VMEM((te, tt), BF16), pltpu.VMEM((te, tt), BF16), pltpu.VMEM((te, tt), BF16),
                        pltpu.VMEM((rows, tt), BF16), pltpu.VMEM((rows, tt), BF16)],
        compiler_params=_params(("arbitrary", "arbitrary", "arbitrary")),
        name="peer_dense",
    )(xn, u, vt, rank2, e2t, split(nsel), split(coef))


def _final_kernel(x1_ref, pt_ref, mod_ref, gf_ref, o_ref):
    x2 = x1_ref[...] + mod_ref[0:1, :] * pt_ref[...].T
    ms = jnp.mean(x2 * x2, axis=-1, keepdims=True)
    o_ref[...] = x2 * lax.rsqrt(ms + NORM_EPS) * gf_ref[...]


def _final(x1, peer_t, mod, gf, tm):
    b, l, d = x1.shape
    nt = l // tm
    tok = pl.BlockSpec((None, tm, d), lambda bi, i: (bi, i, 0))
    return pl.pallas_call(
        _final_kernel,
        grid=(b, nt),
        in_specs=[tok,
                  pl.BlockSpec((d, tm), lambda bi, i: (0, bi * nt + i)),
                  pl.BlockSpec((None, 8, d), lambda bi, i: (bi, 0, 0)),
                  pl.BlockSpec((1, d), lambda bi, i: (0, 0))],
        out_specs=tok,
        out_shape=jax.ShapeDtypeStruct((b, l, d), F32),
        compiler_params=_params(("arbitrary", "arbitrary")),
        name="final_norm",
    )(x1, peer_t, mod, gf)


def _rope_tables(l):
    n_freq = ATT_DK // 4
    inv = ROPE_BASE ** (-2.0 * jnp.arange(n_freq, dtype=F32) / (ATT_DK // 2))
    t = jnp.arange(l, dtype=jnp.int32)
    ang_r = (t // GRID_W).astype(F32)[:, None] * inv[None, :]
    ang_c = (t % GRID_W).astype(F32)[:, None] * inv[None, :]
    cos = jnp.concatenate([jnp.cos(ang_r)] * 2 + [jnp.cos(ang_c)] * 2, axis=-1)
    sin = jnp.concatenate([-jnp.sin(ang_r), jnp.sin(ang_r), -jnp.sin(ang_c), jnp.sin(ang_c)], axis=-1)
    return jnp.tile(cos, (1, 2)), jnp.tile(sin, (1, 2))


def _pick(n, pref):
    for t in pref:
        if n % t == 0:
            return t
    return n


def kernel(x, c, ctx, c_ctx, ada_w, ada_b, norm1_g, norm2_g, w_in, lambda_q1, lambda_k1, lambda_q2, lambda_k2, subln_g, w_attn_up, ssm_a_re, ssm_a_im, ssm_log_dt, ssm_b_re, ssm_b_im, ssm_c_re, ssm_c_im, ssm_d, w_glu, b_glu, w_ssm_up, w_out, peer_w_query, peer_sub_k1, peer_sub_k2, peer_u, peer_v, final_norm_g):
    assert ada_w.shape[0] == 1, "single-layer block"
    b, l, d = x.shape
    nc = ctx.shape[1]
    assert d == D_MODEL and b + 1 <= 8
    ch = SSM_CHUNK

    cin = jnp.concatenate([c, c_ctx[None, :], jnp.zeros((7 - b, d), F32)], axis=0)
    mod = _adaln(cin, ada_w[0], ada_b[0][None, :])
    sh1, sc1, g1, sh2, sc2, g2 = [mod[:, k * d:(k + 1) * d] for k in range(6)]
    mod1 = jnp.stack([sh1[:b], sc1[:b]], axis=1)
    mod1c = jnp.stack([sh1[b], sc1[b]], axis=0)
    zpad = jnp.zeros((b, 5, d), F32)
    mod2 = jnp.concatenate([jnp.stack([g1[:b], sh2[:b], sc2[:b]], axis=1), zpad], axis=1)
    mod3 = jnp.concatenate([g2[:b, None, :], jnp.zeros((b, 7, d), F32)], axis=1)

    w = w_in[0]
    n_q = 2 * ATT_HEADS * ATT_DK
    o_k, o_v, o_u, o_g = n_q, 2 * n_q, 2 * n_q + ATT_HEADS * ATT_DV, 2 * n_q + ATT_HEADS * ATT_DV + SSM_W
    wq, wk, wv, wu, wg = [w[:, a:e].astype(BF16) for a, e in
                          ((0, o_k), (o_k, o_v), (o_v, o_u), (o_u, o_g), (o_g, w.shape[1]))]
    cos, sin = _rope_tables(l)
    g1n = norm1_g[0][None, :]

    tm = _pick(l, (512, 256, 128))
    q, k, v, u, sg = _inproj_x(x, mod1, g1n, cos, sin, wq, wk, wv, wu, wg, tm)
    kc, vc, uc = _inproj_c(ctx, mod1c, g1n, wk, wv, wu)

    row = lambda a: a[0][None, :].astype(F32)
    attn = _attention(q, kc, vc, k, v, row(lambda_q1), row(lambda_k1), row(lambda_q2), row(lambda_k2),
                      row(subln_g), _pick(l, (256, 128)), _pick(l, (256, 128)))

    nsteps = max(1, math.ceil(math.log2((l + nc) // ch)))
    sw = _ssm_weights(ssm_a_re[0], ssm_a_im[0], ssm_log_dt[0], ssm_b_re[0], ssm_b_im[0],
                      ssm_c_re[0], ssm_c_im[0], ssm_d[0], nsteps)
    to_groups = lambda a, n: jnp.transpose(a.reshape(b, n // ch, ch, SSM_GROUPS, SSM_GROUP),
                                           (3, 1, 0, 2, 4)).reshape(SSM_GROUPS, (n // ch) * b, ch * SSM_GROUP)
    yg = _ssm(to_groups(u, l), to_groups(uc, nc), sw, b, nsteps)
    y = jnp.transpose(yg.reshape(SSM_GROUPS, l // ch, b, ch, SSM_GROUP), (2, 1, 3, 0, 4)).reshape(b, l, SSM_W)

    half = PEER_DKEY // 2
    zk = jnp.zeros((PEER_NKEYS, half), F32)
    k1p = jnp.concatenate([peer_sub_k1[0], zk], axis=1).astype(BF16)
    k2p = jnp.concatenate([zk, peer_sub_k2[0]], axis=1).astype(BF16)
    x1, xn2, s1t, s2t = _merge(x, attn, y, sg, mod2, norm2_g[0][None, :],
                               w_glu[0].astype(BF16), b_glu[0][None, :], w_attn_up[0].astype(BF16),
                               w_ssm_up[0].astype(BF16), w_out[0].astype(BF16), peer_w_query[0].astype(BF16),
                               k1p, k2p, tm)

    nsel, coef, rank2, e2t = _route(s1t, s2t, _pick(b * l, (256, 128)))
    peer_t = _peer(xn2, peer_u[0].astype(BF16), peer_v[0].T.astype(BF16), rank2, e2t, nsel, coef,
                   _pick(l, (1024, 512, 256, 128)), 1024)
    return _final(x1, peer_t, mod3, final_norm_g[None, :], tm)
```

```python
import functools
import math

import jax
import jax.numpy as jnp
from jax import lax
from jax.experimental import pallas as pl
from jax.experimental.pallas import tpu as pltpu

F32 = jnp.float32
BF16 = jnp.bfloat16
HIGHEST = lax.Precision.HIGHEST

LANES = 128
MXU_WIDTH = 256
VMEM_LIMIT = 56 * 1024 * 1024

D_MODEL = 1024
NORM_EPS = 1e-6
GRID_W = 64
ATT_HEADS = 8
ATT_DK = 64
ATT_DV = 2 * ATT_DK
ATT_Q_SCALE = ATT_DK ** -0.5 * math.log2(math.e)
ROPE_BASE = 10000.0
SSM_GROUP = 16
SSM_W = D_MODEL // 2
SSM_GROUPS = SSM_W // SSM_GROUP
SSM_STATE = 64
SSM_CHUNK = 16
PEER_HEADS = 8
PEER_NKEYS = 128
PEER_EXPERTS = PEER_NKEYS * PEER_NKEYS
PEER_TOPK = 16
PEER_DKEY = 128
LAMBDA_INIT = 0.8 - 0.6 * math.exp(-0.3 * 0)

NEG_INF = float("-inf")


def _dot(a, b, precision=None):
    return jnp.dot(a, b, preferred_element_type=F32, precision=precision)


def _dot_nt(a, b):
    return lax.dot_general(a, b, (((1,), (1,)), ((), ())), preferred_element_type=F32)


def _sigmoid(z):
    return 1.0 / (1.0 + jnp.exp(-z))


def _gelu_tanh(z):
    return 0.5 * z * (1.0 + jnp.tanh(math.sqrt(2.0 / math.pi) * (z + 0.044715 * (z * z * z))))


def _norm_mod(xf, g, shift, scale):
    ms = jnp.mean(xf * xf, axis=-1, keepdims=True)
    xn = xf * lax.rsqrt(ms + NORM_EPS) * g
    return xn * (1.0 + scale) + shift


def _params(sem):
    return pltpu.CompilerParams(dimension_semantics=sem, vmem_limit_bytes=VMEM_LIMIT)


def _const_spec(shape):
    nd = len(shape)
    return pl.BlockSpec(shape, lambda *_: (0,) * nd, pipeline_mode=pl.Buffered(1))


def _adaln_kernel(c_ref, w_ref, b_ref, o_ref):
    a = c_ref[...]
    act = a * _sigmoid(a)
    o_ref[...] = _dot(act, w_ref[...], precision=HIGHEST) + b_ref[...]


def _adaln(cin, w, b):
    n = w.shape[1]
    tn = 1536
    return pl.pallas_call(
        _adaln_kernel,
        grid=(n // tn,),
        in_specs=[pl.BlockSpec((8, D_MODEL), lambda j: (0, 0)),
                  pl.BlockSpec((D_MODEL, tn), lambda j: (0, j)),
                  pl.BlockSpec((1, tn), lambda j: (0, j))],
        out_specs=pl.BlockSpec((8, tn), lambda j: (0, j)),
        out_shape=jax.ShapeDtypeStruct((8, n), F32),
        compiler_params=_params(("arbitrary",)),
        name="adaln",
    )(cin, w, b)


def _rope_store(h, cos, sin, low, scale, o_ref):
    for hh in range(ATT_HEADS):
        blk = h[:, hh * LANES:(hh + 1) * LANES]
        sw = jnp.where(low, pltpu.roll(blk, LANES - 16, 1), pltpu.roll(blk, 16, 1))
        o_ref[:, hh * LANES:(hh + 1) * LANES] = ((blk * cos + sw * sin) * scale).astype(o_ref.dtype)


def _inproj_x_kernel(x_ref, mod_ref, g_ref, cos_ref, sin_ref, wq_ref, wk_ref, wv_ref, wu_ref, wg_ref,
                     q_ref, k_ref, v_ref, u_ref, sg_ref):
    xb = _norm_mod(x_ref[...], g_ref[...], mod_ref[0:1, :], mod_ref[1:2, :]).astype(BF16)
    cos = cos_ref[...]
    sin = sin_ref[...]
    lane = lax.broadcasted_iota(jnp.int32, (1, LANES), 1)
    low = (lane & 31) < 16
    _rope_store(_dot(xb, wq_ref[...]), cos, sin, low, ATT_Q_SCALE, q_ref)
    _rope_store(_dot(xb, wk_ref[...]), cos, sin, low, 1.0, k_ref)
    v_ref[...] = _dot(xb, wv_ref[...]).astype(v_ref.dtype)
    u_ref[...] = _dot(xb, wu_ref[...])
    sg_ref[...] = _sigmoid(_dot(xb, wg_ref[...])).astype(sg_ref.dtype)


def _inproj_x(x, mod, g, cos, sin, wq, wk, wv, wu, wg, tm):
    b, l, d = x.shape
    tok = lambda n: pl.BlockSpec((None, tm, n), lambda bi, i: (bi, i, 0))
    return pl.pallas_call(
        _inproj_x_kernel,
        grid=(b, l // tm),
        in_specs=[tok(d),
                  pl.BlockSpec((None, 2, d), lambda bi, i: (bi, 0, 0)),
                  pl.BlockSpec((1, d), lambda bi, i: (0, 0)),
                  pl.BlockSpec((tm, LANES), lambda bi, i: (i, 0)),
                  pl.BlockSpec((tm, LANES), lambda bi, i: (i, 0)),
                  _const_spec(wq.shape), _const_spec(wk.shape), _const_spec(wv.shape),
                  _const_spec(wu.shape), _const_spec(wg.shape)],
        out_specs=[tok(d), tok(d), tok(d), tok(SSM_W), tok(2 * d)],
        out_shape=[jax.ShapeDtypeStruct((b, l, d), BF16),
                   jax.ShapeDtypeStruct((b, l, d), BF16),
                   jax.ShapeDtypeStruct((b, l, d), BF16),
                   jax.ShapeDtypeStruct((b, l, SSM_W), F32),
                   jax.ShapeDtypeStruct((b, l, 2 * d), BF16)],
        compiler_params=_params(("arbitrary", "arbitrary")),
        name="inproj_x",
    )(x, mod, g, cos, sin, wq, wk, wv, wu, wg)


def _inproj_c_kernel(x_ref, mod_ref, g_ref, wk_ref, wv_ref, wu_ref, k_ref, v_ref, u_ref):
    xb = _norm_mod(x_ref[...], g_ref[...], mod_ref[0:1, :], mod_ref[1:2, :]).astype(BF16)
    k_ref[...] = _dot(xb, wk_ref[...]).astype(k_ref.dtype)
    v_ref[...] = _dot(xb, wv_ref[...]).astype(v_ref.dtype)
    u_ref[...] = _dot(xb, wu_ref[...])


def _inproj_c(ctx, mod, g, wk, wv, wu):
    b, n, d = ctx.shape
    tok = lambda w: pl.BlockSpec((None, n, w), lambda bi: (bi, 0, 0))
    return pl.pallas_call(
        _inproj_c_kernel,
        grid=(b,),
        in_specs=[tok(d),
                  pl.BlockSpec((2, d), lambda bi: (0, 0)),
                  pl.BlockSpec((1, d), lambda bi: (0, 0)),
                  _const_spec(wk.shape), _const_spec(wv.shape), _const_spec(wu.shape)],
        out_specs=[tok(d), tok(d), tok(SSM_W)],
        out_shape=[jax.ShapeDtypeStruct((b, n, d), BF16),
                   jax.ShapeDtypeStruct((b, n, d), BF16),
                   jax.ShapeDtypeStruct((b, n, SSM_W), F32)],
        compiler_params=_params(("arbitrary",)),
        name="inproj_ctx",
    )(ctx, mod, g, wk, wv, wu)


def _attn_chunks(kc_ref, kx_ref, tk):
    nc, nx = kc_ref.shape[0], kx_ref.shape[0]
    return [(False, 0, nc, 0)] + [(True, j * tk, tk, nc + j * tk) for j in range(nx // tk)]


def _attn_step(q_ref, kc_ref, vc_ref, kx_ref, vx_ref, lq1_ref, lk1_ref, lq2_ref, lk2_ref, sg_ref, o_ref,
               s_new_ref, m_new_ref, s_old_ref, m_old_ref, *, tq, tk, score, finish):
    if score:
        q = q_ref[...]
        lane = lax.broadcasted_iota(jnp.int32, (1, LANES), 1)
        zero = jnp.zeros_like(q)
        qs = jnp.concatenate([jnp.where(lane < ATT_DK, q, zero), jnp.where(lane >= ATT_DK, q, zero)], axis=0)
        m_lane = jnp.full((2 * tq, LANES), NEG_INF, F32)
    if finish:
        m = jnp.broadcast_to(jnp.max(m_old_ref[...], axis=1, keepdims=True), (2 * tq, LANES))
        acc = jnp.zeros((2 * tq, 2 * ATT_DV), F32)
    for latent, off, n, col in _attn_chunks(kc_ref, kx_ref, tk):
        k_ref, v_ref = (kx_ref, vx_ref) if latent else (kc_ref, vc_ref)
        if score:
            s = _dot_nt(qs, k_ref[off:off + n, :])
            s_new_ref[:, col:col + n] = s
            smax = s[:, 0:LANES]
            for g in range(1, n // LANES):
                smax = jnp.maximum(smax, s[:, g * LANES:(g + 1) * LANES])
            m_lane = jnp.maximum(m_lane, smax)
        if finish:
            mc = m
            if score:
                mc = m + jnp.minimum(jnp.abs(smax), 0.0)
            ps = [jnp.exp2(s_old_ref[:, col + g * LANES:col + (g + 1) * LANES] - mc).astype(BF16)
                  for g in range(n // LANES)]
            v1 = jnp.concatenate([v_ref[off:off + n, :], jnp.ones((n, ATT_DV), BF16)], axis=1)
            acc = acc + _dot(jnp.concatenate(ps, axis=1), v1)
    if score:
        m_new_ref[...] = m_lane
    if finish:
        o = acc[:, :ATT_DV] / acc[:, ATT_DV:]
        lam = (jnp.exp(jnp.sum(lq1_ref[...] * lk1_ref[...], axis=1, keepdims=True))
               - jnp.exp(jnp.sum(lq2_ref[...] * lk2_ref[...], axis=1, keepdims=True)) + LAMBDA_INIT)
        o = o[:tq] - lam * o[tq:]
        ms = jnp.mean(o * o, axis=-1, keepdims=True)
        o = o * lax.rsqrt(ms + NORM_EPS) * sg_ref[...] * (1.0 - LAMBDA_INIT)
        o_ref[...] = o.astype(o_ref.dtype)


def _attn_kernel(q_ref, kc_ref, vc_ref, kx_ref, vx_ref, lq1_ref, lk1_ref, lq2_ref, lk2_ref, sg_ref, o_ref,
                 s0_ref, s1_ref, m0_ref, m1_ref, *, tq, tk):
    i = pl.program_id(2)
    step = functools.partial(_attn_step, q_ref, kc_ref, vc_ref, kx_ref, vx_ref, lq1_ref, lk1_ref, lq2_ref, lk2_ref,
                             sg_ref, o_ref, tq=tq, tk=tk)

    @pl.when(i == 0)
    def _():
        step(s0_ref, m0_ref, s1_ref, m1_ref, score=True, finish=False)

    last = pl.num_programs(2) - 1

    @pl.when(jnp.logical_and(i % 2 == 1, i < last))
    def _():
        step(s1_ref, m1_ref, s0_ref, m0_ref, score=True, finish=True)

    @pl.when(jnp.logical_and(jnp.logical_and(i > 0, i % 2 == 0), i < last))
    def _():
        step(s0_ref, m0_ref, s1_ref, m1_ref, score=True, finish=True)

    @pl.when(jnp.logical_and(i % 2 == 1, i == last))
    def _():
        step(s1_ref, m1_ref, s0_ref, m0_ref, score=False, finish=True)

    @pl.when(jnp.logical_and(jnp.logical_and(i > 0, i % 2 == 0), i == last))
    def _():
        step(s0_ref, m0_ref, s1_ref, m1_ref, score=False, finish=True)


def _attention(q, kc, vc, kx, vx, lq1, lk1, lq2, lk2, subln_g, tq, tk):
    b, l, d = q.shape
    nc = kc.shape[1]
    nq = l // tq
    vec = lambda n: pl.BlockSpec((1, n), lambda bi, h, i: (0, 0))
    return pl.pallas_call(
        functools.partial(_attn_kernel, tq=tq, tk=tk),
        grid=(b, ATT_HEADS, nq + 1),
        in_specs=[pl.BlockSpec((None, tq, LANES), lambda bi, h, i: (bi, jnp.minimum(i, nq - 1), h)),
                  pl.BlockSpec((None, nc, LANES), lambda bi, h, i: (bi, 0, h)),
                  pl.BlockSpec((None, nc, LANES), lambda bi, h, i: (bi, 0, h)),
                  pl.BlockSpec((None, l, LANES), lambda bi, h, i: (bi, 0, h)),
                  pl.BlockSpec((None, l, LANES), lambda bi, h, i: (bi, 0, h)),
                  vec(ATT_DK), vec(ATT_DK), vec(ATT_DK), vec(ATT_DK), vec(ATT_DV)],
        out_specs=pl.BlockSpec((None, tq, LANES), lambda bi, h, i: (bi, jnp.maximum(i - 1, 0), h)),
        out_shape=jax.ShapeDtypeStruct((b, l, d), BF16),
        scratch_shapes=[pltpu.VMEM((2 * tq, nc + l), F32), pltpu.VMEM((2 * tq, nc + l), F32),
                        pltpu.VMEM((2 * tq, LANES), F32), pltpu.VMEM((2 * tq, LANES), F32)],
        compiler_params=_params(("arbitrary", "arbitrary", "arbitrary")),
        name="diff_attn",
    )(q, kc, vc, kx, vx, lq1, lk1, lq2, lk2, subln_g)


def _ssm_weights(a_re, a_im, log_dt, b_re, b_im, c_re, c_im, d_skip, nsteps):
    ch, g, n, p = SSM_CHUNK, SSM_GROUPS, SSM_STATE, SSM_GROUP
    ar = a_re.astype(F32)
    ai = a_im.astype(F32)
    dt = jnp.exp(log_dt.astype(F32))[..., None]
    lam = ar * dt
    theta = ai * dt
    mag = jnp.exp(lam)
    abar_re = mag * jnp.cos(theta)
    abar_im = mag * jnp.sin(theta)
    den = ar * ar + ai * ai
    nr = abar_re - 1.0
    ni = abar_im
    fr = (nr * ar + ni * ai) / den
    fi = (ni * ar - nr * ai) / den
    br = b_re.astype(F32)
    bi = b_im.astype(F32)
    bb_re = fr[..., None] * br - fi[..., None] * bi
    bb_im = fr[..., None] * bi + fi[..., None] * br
    cr = c_re.astype(F32)
    ci = c_im.astype(F32)

    j = jnp.arange(ch + 1, dtype=F32)[:, None, None, None]
    pr = jnp.exp(lam[None] * j) * jnp.cos(theta[None] * j)
    pi = jnp.exp(lam[None] * j) * jnp.sin(theta[None] * j)

    cb_re = cr[..., None] * bb_re[:, :, None] - ci[..., None] * bb_im[:, :, None]
    cb_im = cr[..., None] * bb_im[:, :, None] + ci[..., None] * bb_re[:, :, None]
    kker = (jnp.einsum("dgpnq,ldgn->dlgpq", cb_re, pr[:ch], precision=HIGHEST)
            - jnp.einsum("dgpnq,ldgn->dlgpq", cb_im, pi[:ch], precision=HIGHEST))
    ii = jnp.arange(ch)[:, None]
    jj = jnp.arange(ch)[None, :]
    kf = jnp.where((jj >= ii)[..., None, None, None], kker[0][jnp.clip(jj - ii, 0, ch - 1)], 0.0)
    kb = jnp.where((ii >= jj)[..., None, None, None], kker[1][jnp.clip(ii - jj, 0, ch - 1)], 0.0)
    mm = jnp.transpose(kf + kb, (2, 0, 4, 1, 3)).reshape(g, ch * p, ch * p)
    dvec = jnp.tile(d_skip.astype(F32).reshape(g, 1, p), (1, ch, 1)).reshape(g, ch * p)
    mm = mm + jnp.eye(ch * p, dtype=F32)[None] * dvec[:, None, :]

    def bpow(d, idx):
        pre = pr[idx, d][:, :, :, None]
        pim = pi[idx, d][:, :, :, None]
        re = pre * bb_re[d][None] - pim * bb_im[d][None]
        im = pre * bb_im[d][None] + pim * bb_re[d][None]
        both = jnp.stack([re, im], axis=0)
        return jnp.transpose(both, (2, 1, 4, 0, 3)).reshape(g, ch * p, 2 * n)

    def cpow(d, idx):
        pre = pr[idx, d][:, :, None, :]
        pim = pi[idx, d][:, :, None, :]
        ca_re = cr[d][None] * pre - ci[d][None] * pim
        ca_im = cr[d][None] * pim + ci[d][None] * pre
        both = jnp.stack([ca_re, -ca_im], axis=0)
        return jnp.transpose(both, (2, 0, 4, 1, 3)).reshape(g, 2 * n, ch * p)

    ar_idx = jnp.arange(ch)
    bpf = bpow(0, ch - 1 - ar_idx)
    bpb = bpow(1, ar_idx)
    cpf = cpow(0, ar_idx + 1)
    cpb = cpow(1, ch - ar_idx)

    re, im = pr[ch], pi[ch]
    rows1, rows2 = [], []
    for _ in range(nsteps):
        rows1.append(jnp.concatenate([re, re], axis=-1))
        rows2.append(jnp.concatenate([-im, im], axis=-1))
        re, im = re * re - im * im, 2.0 * re * im
    pad = [jnp.zeros_like(rows1[0])] * (16 - nsteps)
    coef = jnp.stack([jnp.stack(rows1 + pad, axis=2), jnp.stack(rows2 + pad, axis=2)], axis=2)
    coef = jnp.transpose(coef, (1, 0, 2, 3, 4)).reshape(g, 4, 16, 2 * n)
    return mm, bpf, bpb, cpf, cpb, coef


def _ssm_kernel(ux_ref, uc_ref, m_ref, bpf_ref, bpb_ref, cpf_ref, cpb_ref, coef_ref, y_ref, *, nb, nsteps):
    ux = ux_ref[...]
    uc = uc_ref[...]
    rx, rc = ux.shape[0], uc.shape[0]
    r = rx + rc
    bpf = bpf_ref[...]
    bpb = bpb_ref[...]
    hf = jnp.concatenate([_dot(uc, bpf, HIGHEST), _dot(ux, bpf, HIGHEST)], axis=0)
    gb = jnp.concatenate([_dot(ux, bpb, HIGHEST), _dot(uc, bpb, HIGHEST)], axis=0)
    row = lax.broadcasted_iota(jnp.int32, (r, 1), 0)
    half = SSM_STATE
    for k in range(nsteps):
        s = nb << k
        if s >= r:
            break
        sh = jnp.where(row >= s, pltpu.roll(hf, s, 0), 0.0)
        hf = hf + coef_ref[0, k:k + 1, :] * sh + coef_ref[1, k:k + 1, :] * pltpu.roll(sh, half, 1)
        sh = jnp.where(row < r - s, pltpu.roll(gb, r - s, 0), 0.0)
        gb = gb + coef_ref[2, k:k + 1, :] * sh + coef_ref[3, k:k + 1, :] * pltpu.roll(sh, half, 1)
    hprev = jnp.where(row >= nb, pltpu.roll(hf, nb, 0), 0.0)[rc:]
    gnext = jnp.where(row < r - nb, pltpu.roll(gb, r - nb, 0), 0.0)[:rx]
    y_ref[...] = (_dot(ux, m_ref[...], HIGHEST) + _dot(hprev, cpf_ref[...], HIGHEST)
                  + _dot(gnext, cpb_ref[...], HIGHEST))


def _ssm(ux, uc, weights, nb, nsteps):
    g, rx, w = ux.shape
    rc = uc.shape[1]
    mm, bpf, bpb, cpf, cpb, coef = weights
    per_g = lambda a: pl.BlockSpec((None,) + a.shape[1:], lambda gi: (gi,) + (0,) * (a.ndim - 1))
    return pl.pallas_call(
        functools.partial(_ssm_kernel, nb=nb, nsteps=nsteps),
        grid=(g,),
        in_specs=[per_g(ux), per_g(uc), per_g(mm), per_g(bpf), per_g(bpb), per_g(cpf), per_g(cpb), per_g(coef)],
        out_specs=pl.BlockSpec((None, rx, w), lambda gi: (gi, 0, 0)),
        out_shape=jax.ShapeDtypeStruct((g, rx, w), F32),
        compiler_params=_params(("arbitrary",)),
        name="s5_chunked",
    )(ux, uc, mm, bpf, bpb, cpf, cpb, coef)


def _merge_kernel(x_ref, attn_ref, y_ref, sg_ref, mod_ref, g2_ref, wglu_ref, bglu_ref, wau_ref, wsu_ref, wout_ref,
                  wqry_ref, k1_ref, k2_ref, x1_ref, xn_ref, s1_ref, s2_ref):
    d = D_MODEL
    yg = _gelu_tanh(y_ref[...])
    ysg = yg * _sigmoid(_dot(yg.astype(BF16), wglu_ref[...]) + bglu_ref[...])
    a_up = _dot(attn_ref[...], wau_ref[...])
    s_up = _dot(ysg.astype(BF16), wsu_ref[...])
    mix_in = sg_ref[:, :d].astype(F32) * a_up + sg_ref[:, d:].astype(F32) * s_up
    mix = _dot(mix_in.astype(BF16), wout_ref[...])
    x1 = x_ref[...] + mod_ref[0:1, :] * mix
    x1_ref[...] = x1
    xn = _norm_mod(x1, g2_ref[...], mod_ref[1:2, :], mod_ref[2:3, :]).astype(BF16)
    xn_ref[...] = xn
    qb = _dot(xn, wqry_ref[...]).astype(BF16)
    for h in range(PEER_HEADS):
        qh = qb[:, h * LANES:(h + 1) * LANES]
        s1_ref[h * LANES:(h + 1) * LANES, :] = _dot_nt(k1_ref[...], qh)
        s2_ref[h * LANES:(h + 1) * LANES, :] = _dot_nt(k2_ref[...], qh)


def _merge(x, attn, y, sg, mod, g2, wglu, bglu, wau, wsu, wout, wqry, k1p, k2p, tm):
    b, l, d = x.shape
    ntok = b * l
    nt = l // tm
    tok = lambda n: pl.BlockSpec((None, tm, n), lambda bi, i: (bi, i, 0))
    tokT = pl.BlockSpec((PEER_HEADS * PEER_NKEYS, tm), lambda bi, i: (0, bi * nt + i))
    return pl.pallas_call(
        _merge_kernel,
        grid=(b, nt),
        in_specs=[tok(d), tok(d), tok(SSM_W), tok(2 * d),
                  pl.BlockSpec((None, 8, d), lambda bi, i: (bi, 0, 0)),
                  pl.BlockSpec((1, d), lambda bi, i: (0, 0)),
                  _const_spec(wglu.shape), _const_spec(bglu.shape), _const_spec(wau.shape),
                  _const_spec(wsu.shape), _const_spec(wout.shape), _const_spec(wqry.shape),
                  _const_spec(k1p.shape), _const_spec(k2p.shape)],
        out_specs=[tok(d), tok(d), tokT, tokT],
        out_shape=[jax.ShapeDtypeStruct((b, l, d), F32),
                   jax.ShapeDtypeStruct((b, l, d), BF16),
                   jax.ShapeDtypeStruct((PEER_HEADS * PEER_NKEYS, ntok), F32),
                   jax.ShapeDtypeStruct((PEER_HEADS * PEER_NKEYS, ntok), F32)],
        compiler_params=_params(("arbitrary", "arbitrary")),
        name="merge_peer_query",
    )(x, attn, y, sg, mod, g2, wglu, bglu, wau, wsu, wout, wqry, k1p, k2p)


CAND_ROWS = 80


def _pair_candidates(v1, v2):
    rows8 = lax.broadcasted_iota(jnp.int32, (8, 1), 0)
    blocks = [v1[0:1, :] + v2]
    for a in range(1, 8):
        blocks.append(jnp.where(rows8 < PEER_TOPK // (a + 1), v1[a:a + 1, :] + v2[0:8, :], NEG_INF))
    blocks.append(v1[8:16, :] + v2[0:1, :])
    return jnp.concatenate(blocks, axis=0)


def _route_kernel(s1_ref, s2_ref, nsel_ref, coef_ref, rank2_ref, e2_ref,
                  c1_ref, c2_ref, rk_ref, v1_ref, v2_ref, cand_ref, tau_ref):
    nh, nk, topk = PEER_HEADS, PEER_NKEYS, PEER_TOPK
    keys = lambda h: slice(h * nk, (h + 1) * nk)
    tops = lambda h: slice(h * topk, (h + 1) * topk)
    cands = lambda h: slice(h * CAND_ROWS, (h + 1) * CAND_ROWS)
    c1_ref[...] = s1_ref[...]
    c2_ref[...] = s2_ref[...]
    rk_ref[...] = jnp.full(rk_ref.shape, float(topk), F32)
    v1_ref[...] = jnp.zeros_like(v1_ref)
    v2_ref[...] = jnp.zeros_like(v2_ref)
    rows16 = lax.broadcasted_iota(jnp.int32, (topk, 1), 0)

    def extract(k, _):
        for h in range(nh):
            cur = c1_ref[keys(h), :]
            m = jnp.max(cur, axis=0, keepdims=True)
            c1_ref[keys(h), :] = jnp.where(cur >= m, NEG_INF, cur)
            v1_ref[tops(h), :] = jnp.where(rows16 == k, m, v1_ref[tops(h), :])
            cur = c2_ref[keys(h), :]
            m = jnp.max(cur, axis=0, keepdims=True)
            hit = cur >= m
            c2_ref[keys(h), :] = jnp.where(hit, NEG_INF, cur)
            rk_ref[keys(h), :] = jnp.where(hit, k.astype(F32), rk_ref[keys(h), :])
            v2_ref[tops(h), :] = jnp.where(rows16 == k, m, v2_ref[tops(h), :])
        return 0

    lax.fori_loop(0, topk, extract, 0)

    for h in range(nh):
        cand_ref[cands(h), :] = _pair_candidates(v1_ref[tops(h), :], v2_ref[tops(h), :])

    def knock(k, _):
        for h in range(nh):
            cur = cand_ref[cands(h), :]
            m = jnp.max(cur, axis=0, keepdims=True)
            cand_ref[cands(h), :] = jnp.where(cur >= m, NEG_INF, cur)
            tau_ref[h:h + 1, :] = m
        return 0

    lax.fori_loop(0, topk, knock, 0)

    for h in range(nh):
        v1 = v1_ref[tops(h), :]
        v2 = v2_ref[tops(h), :]
        tau = tau_ref[h:h + 1, :]
        cand = _pair_candidates(v1, v2)
        mx = v1[0:1, :] + v2[0:1, :]
        z = jnp.sum(jnp.where(cand >= tau, jnp.exp(cand - mx), 0.0), axis=0, keepdims=True)
        s1 = s1_ref[keys(h), :]
        nsel = jnp.zeros_like(s1)
        for bpos in range(topk):
            nsel = nsel + jnp.where(s1 + v2[bpos:bpos + 1, :] >= tau, 1.0, 0.0)
        nsel_ref[keys(h), :] = jnp.where(s1 >= v1[topk - 1:topk, :], nsel, 0.0)
        coef_ref[keys(h), :] = jnp.exp(s1 - v1[0:1, :]) / z
        rank2_ref[keys(h), :] = rk_ref[keys(h), :].astype(rank2_ref.dtype)
        e2_ref[keys(h), :] = jnp.exp(s2_ref[keys(h), :] - v2[0:1, :]).astype(e2_ref.dtype)


def _route(s1t, s2t, tt):
    rows, ntok = s1t.shape
    spec = pl.BlockSpec((rows, tt), lambda i: (0, i))
    f32 = jax.ShapeDtypeStruct((rows, ntok), F32)
    bf16 = jax.ShapeDtypeStruct((rows, ntok), BF16)
    big = pltpu.VMEM((rows, tt), F32)
    small = pltpu.VMEM((PEER_HEADS * PEER_TOPK, tt), F32)
    return pl.pallas_call(
        _route_kernel,
        grid=(ntok // tt,),
        in_specs=[spec, spec],
        out_specs=[spec, spec, spec, spec],
        out_shape=[f32, f32, bf16, bf16],
        scratch_shapes=[big, big, big, small, small,
                        pltpu.VMEM((PEER_HEADS * CAND_ROWS, tt), F32), pltpu.VMEM((PEER_HEADS, tt), F32)],
        compiler_params=_params(("arbitrary",)),
        name="peer_route",
    )(s1t, s2t)


def _peer_step(xn_ref, u_ref, vt_ref, rank2_ref, e2_ref, nsel_ref, coef_ref, act_ref, gate_ref, acc_ref, *, te, tt):
    act_ref[...] = _dot_nt(u_ref[...], xn_ref[...]).astype(BF16)
    nsub = te // PEER_NKEYS
    zero = jnp.zeros((PEER_NKEYS, LANES), BF16)
    for lb in range(tt // LANES):
        ls = slice(lb * LANES, (lb + 1) * LANES)
        ns = [nsel_ref[h, :, ls].astype(BF16) for h in range(PEER_HEADS)]
        cf = [coef_ref[h, :, ls].astype(BF16) for h in range(PEER_HEADS)]
        for r in range(nsub):
            rs = slice(r * PEER_NKEYS, (r + 1) * PEER_NKEYS)
            w = zero
            for h in range(PEER_HEADS):
                hs = slice(h * PEER_NKEYS, (h + 1) * PEER_NKEYS)
                sel = rank2_ref[hs, ls] < ns[h][r:r + 1, :]
                w = w + jnp.where(sel, e2_ref[hs, ls], zero) * cf[h][r:r + 1, :]
            gate_ref[rs, ls] = (w.astype(F32) * _gelu_tanh(act_ref[rs, ls].astype(F32))).astype(BF16)
    acc_ref[...] += _dot(vt_ref[...], gate_ref[...])


def _peer_kernel(xn_ref, u_ref, vt_ref, rank2_ref, e2_ref, nsel_ref, coef_ref, o_ref,
                 act_ref, gate_ref, rank2b_ref, e2b_ref, *, te, tt):
    @pl.when(pl.program_id(2) == 0)
    def _():
        o_ref[...] = jnp.zeros_like(o_ref)
        rank2b_ref[...] = rank2_ref[...]
        e2b_ref[...] = e2_ref[...]

    _peer_step(xn_ref, u_ref, vt_ref, rank2b_ref, e2b_ref, nsel_ref, coef_ref, act_ref, gate_ref, o_ref, te=te, tt=tt)


def _peer(xn, u, vt, rank2, e2t, nsel, coef, tt, te):
    b, l, d = xn.shape
    nt = l // tt
    ne = PEER_EXPERTS // te
    nsub = te // PEER_NKEYS
    rows = PEER_HEADS * PEER_NKEYS
    ntok = b * l
    once = dict(pipeline_mode=pl.Buffered(1))
    tokT = pl.BlockSpec((rows, tt), lambda bi, i, j: (0, bi * nt + i), **once)
    chunk_rows = pl.BlockSpec((PEER_HEADS, None, nsub, tt), lambda bi, i, j: (0, j, 0, bi * nt + i))
    split = lambda a: a.reshape(PEER_HEADS, PEER_NKEYS // nsub, nsub, ntok)
    return pl.pallas_call(
        functools.partial(_peer_kernel, te=te, tt=tt),
        grid=(b, nt, ne),
        in_specs=[pl.BlockSpec((None, tt, d), lambda bi, i, j: (bi, i, 0), **once),
                  pl.BlockSpec((te, d), lambda bi, i, j: (j, 0)),
                  pl.BlockSpec((d, te), lambda bi, i, j: (0, j)),
                  tokT, tokT, chunk_rows, chunk_rows],
        out_specs=pl.BlockSpec((d, tt), lambda bi, i, j: (0, bi * nt + i)),
        out_shape=jax.ShapeDtypeStruct((d, ntok), F32),
        scratch_shapes=[pltpu.VMEM((te, tt), BF16), pltpu.VMEM((te, tt), BF16),
                        pltpu.VMEM((rows, tt), BF16), pltpu.VMEM((rows, tt), BF16)],
        compiler_params=_params(("arbitrary", "arbitrary", "arbitrary")),
        name="peer_dense",
    )(xn, u, vt, rank2, e2t, split(nsel), split(coef))


def _final_kernel(x1_ref, pt_ref, mod_ref, gf_ref, o_ref):
    x2 = x1_ref[...] + mod_ref[0:1, :] * pt_ref[...].T
    ms = jnp.mean(x2 * x2, axis=-1, keepdims=True)
    o_ref[...] = x2 * lax.rsqrt(ms + NORM_EPS) * gf_ref[...]


def _final(x1, peer_t, mod, gf, tm):
    b, l, d = x1.shape
    nt = l // tm
    tok = pl.BlockSpec((None, tm, d), lambda bi, i: (bi, i, 0))
    return pl.pallas_call(
        _final_kernel,
        grid=(b, nt),
        in_specs=[tok,
                  pl.BlockSpec((d, tm), lambda bi, i: (0, bi * nt + i)),
                  pl.BlockSpec((None, 8, d), lambda bi, i: (bi, 0, 0)),
                  pl.BlockSpec((1, d), lambda bi, i: (0, 0))],
        out_specs=tok,
        out_shape=jax.ShapeDtypeStruct((b, l, d), F32),
        compiler_params=_params(("arbitrary", "arbitrary")),
        name="final_norm",
    )(x1, peer_t, mod, gf)


def _rope_tables(l):
    n_freq = ATT_DK // 4
    inv = ROPE_BASE ** (-2.0 * jnp.arange(n_freq, dtype=F32) / (ATT_DK // 2))
    t = jnp.arange(l, dtype=jnp.int32)
    ang_r = (t // GRID_W).astype(F32)[:, None] * inv[None, :]
    ang_c = (t % GRID_W).astype(F32)[:, None] * inv[None, :]
    cos = jnp.concatenate([jnp.cos(ang_r)] * 2 + [jnp.cos(ang_c)] * 2, axis=-1)
    sin = jnp.concatenate([-jnp.sin(ang_r), jnp.sin(ang_r), -jnp.sin(ang_c), jnp.sin(ang_c)], axis=-1)
    return jnp.tile(cos, (1, 2)), jnp.tile(sin, (1, 2))


def _pick(n, pref):
    for t in pref:
        if n % t == 0:
            return t
    return n


def kernel(x, c, ctx, c_ctx, ada_w, ada_b, norm1_g, norm2_g, w_in, lambda_q1, lambda_k1, lambda_q2, lambda_k2, subln_g, w_attn_up, ssm_a_re, ssm_a_im, ssm_log_dt, ssm_b_re, ssm_b_im, ssm_c_re, ssm_c_im, ssm_d, w_glu, b_glu, w_ssm_up, w_out, peer_w_query, peer_sub_k1, peer_sub_k2, peer_u, peer_v, final_norm_g):
    assert ada_w.shape[0] == 1, "single-layer block"
    b, l, d = x.shape
    nc = ctx.shape[1]
    assert d == D_MODEL and b + 1 <= 8
    ch = SSM_CHUNK

    cin = jnp.concatenate([c, c_ctx[None, :], jnp.zeros((7 - b, d), F32)], axis=0)
    mod = _adaln(cin, ada_w[0], ada_b[0][None, :])
    sh1, sc1, g1, sh2, sc2, g2 = [mod[:, k * d:(k + 1) * d] for k in range(6)]
    mod1 = jnp.stack([sh1[:b], sc1[:b]], axis=1)
    mod1c = jnp.stack([sh1[b], sc1[b]], axis=0)
    zpad = jnp.zeros((b, 5, d), F32)
    mod2 = jnp.concatenate([jnp.stack([g1[:b], sh2[:b], sc2[:b]], axis=1), zpad], axis=1)
    mod3 = jnp.concatenate([g2[:b, None, :], jnp.zeros((b, 7, d), F32)], axis=1)

    w = w_in[0]
    n_q = 2 * ATT_HEADS * ATT_DK
    o_k, o_v, o_u, o_g = n_q, 2 * n_q, 2 * n_q + ATT_HEADS * ATT_DV, 2 * n_q + ATT_HEADS * ATT_DV + SSM_W
    wq, wk, wv, wu, wg = [w[:, a:e].astype(BF16) for a, e in
                          ((0, o_k), (o_k, o_v), (o_v, o_u), (o_u, o_g), (o_g, w.shape[1]))]
    cos, sin = _rope_tables(l)
    g1n = norm1_g[0][None, :]

    tm = _pick(l, (512, 256, 128))
    q, k, v, u, sg = _inproj_x(x, mod1, g1n, cos, sin, wq, wk, wv, wu, wg, tm)
    kc, vc, uc = _inproj_c(ctx, mod1c, g1n, wk, wv, wu)

    row = lambda a: a[0][None, :].astype(F32)
    attn = _attention(q, kc, vc, k, v, row(lambda_q1), row(lambda_k1), row(lambda_q2), row(lambda_k2),
                      row(subln_g), _pick(l, (256, 128)), _pick(l, (256, 128)))

    nsteps = max(1, math.ceil(math.log2((l + nc) // ch)))
    sw = _ssm_weights(ssm_a_re[0], ssm_a_im[0], ssm_log_dt[0], ssm_b_re[0], ssm_b_im[0],
                      ssm_c_re[0], ssm_c_im[0], ssm_d[0], nsteps)
    to_groups = lambda a, n: jnp.transpose(a.reshape(b, n // ch, ch, SSM_GROUPS, SSM_GROUP),
                                           (3, 1, 0, 2, 4)).reshape(SSM_GROUPS, (n // ch) * b, ch * SSM_GROUP)
    yg = _ssm(to_groups(u, l), to_groups(uc, nc), sw, b, nsteps)
    y = jnp.transpose(yg.reshape(SSM_GROUPS, l // ch, b, ch, SSM_GROUP), (2, 1, 3, 0, 4)).reshape(b, l, SSM_W)

    half = PEER_DKEY // 2
    zk = jnp.zeros((PEER_NKEYS, half), F32)
    k1p = jnp.concatenate([peer_sub_k1[0], zk], axis=1).astype(BF16)
    k2p = jnp.concatenate([zk, peer_sub_k2[0]], axis=1).astype(BF16)
    x1, xn2, s1t, s2t = _merge(x, attn, y, sg, mod2, norm2_g[0][None, :],
                               w_glu[0].astype(BF16), b_glu[0][None, :], w_attn_up[0].astype(BF16),
                               w_ssm_up[0].astype(BF16), w_out[0].astype(BF16), peer_w_query[0].astype(BF16),
                               k1p, k2p, tm)

    nsel, coef, rank2, e2t = _route(s1t, s2t, _pick(b * l, (256, 128)))
    peer_t = _peer(xn2, peer_u[0].astype(BF16), peer_v[0].T.astype(BF16), rank2, e2t, nsel, coef,
                   _pick(l, (1024, 512, 256, 128)), 1024)
    return _final(x1, peer_t, mod3, final_norm_g[None, :], tm)
```

```python
import functools
import math

import jax
import jax.numpy as jnp
from jax import lax
from jax.experimental import pallas as pl
from jax.experimental.pallas import tpu as pltpu

F32 = jnp.float32
BF16 = jnp.bfloat16
HIGHEST = lax.Precision.HIGHEST

LANES = 128
MXU_WIDTH = 256
VMEM_LIMIT = 56 * 1024 * 1024

D_MODEL = 1024
NORM_EPS = 1e-6
GRID_W = 64
ATT_HEADS = 8
ATT_DK = 64
ATT_DV = 2 * ATT_DK
ATT_Q_SCALE = ATT_DK ** -0.5 * math.log2(math.e)
ROPE_BASE = 10000.0
SSM_GROUP = 16
SSM_W = D_MODEL // 2
SSM_GROUPS = SSM_W // SSM_GROUP
SSM_STATE = 64
SSM_CHUNK = 16
PEER_HEADS = 8
PEER_NKEYS = 128
PEER_EXPERTS = PEER_NKEYS * PEER_NKEYS
PEER_TOPK = 16
PEER_DKEY = 128
LAMBDA_INIT = 0.8 - 0.6 * math.exp(-0.3 * 0)

NEG_INF = float("-inf")


def _dot(a, b, precision=None):
    return jnp.dot(a, b, preferred_element_type=F32, precision=precision)


def _dot_nt(a, b):
    return lax.dot_general(a, b, (((1,), (1,)), ((), ())), preferred_element_type=F32)


def _sigmoid(z):
    return 1.0 / (1.0 + jnp.exp(-z))


def _gelu_tanh(z):
    return 0.5 * z * (1.0 + jnp.tanh(math.sqrt(2.0 / math.pi) * (z + 0.044715 * (z * z * z))))


def _norm_mod(xf, g, shift, scale):
    ms = jnp.mean(xf * xf, axis=-1, keepdims=True)
    xn = xf * lax.rsqrt(ms + NORM_EPS) * g
    return xn * (1.0 + scale) + shift


def _params(sem):
    return pltpu.CompilerParams(dimension_semantics=sem, vmem_limit_bytes=VMEM_LIMIT)


def _const_spec(shape):
    nd = len(shape)
    return pl.BlockSpec(shape, lambda *_: (0,) * nd, pipeline_mode=pl.Buffered(1))


def _adaln_kernel(c_ref, w_ref, b_ref, o_ref):
    a = c_ref[...]
    act = a * _sigmoid(a)
    o_ref[...] = _dot(act, w_ref[...], precision=HIGHEST) + b_ref[...]


def _adaln(cin, w, b):
    n = w.shape[1]
    tn = 1536
    return pl.pallas_call(
        _adaln_kernel,
        grid=(n // tn,),
        in_specs=[pl.BlockSpec((8, D_MODEL), lambda j: (0, 0)),
                  pl.BlockSpec((D_MODEL, tn), lambda j: (0, j)),
                  pl.BlockSpec((1, tn), lambda j: (0, j))],
        out_specs=pl.BlockSpec((8, tn), lambda j: (0, j)),
        out_shape=jax.ShapeDtypeStruct((8, n), F32),
        compiler_params=_params(("arbitrary",)),
        name="adaln",
    )(cin, w, b)


def _rope_store(h, cos, sin, low, scale, o_ref):
    for hh in range(ATT_HEADS):
        blk = h[:, hh * LANES:(hh + 1) * LANES]
        sw = jnp.where(low, pltpu.roll(blk, LANES - 16, 1), pltpu.roll(blk, 16, 1))
        o_ref[:, hh * LANES:(hh + 1) * LANES] = ((blk * cos + sw * sin) * scale).astype(o_ref.dtype)


def _inproj_x_kernel(x_ref, mod_ref, g_ref, cos_ref, sin_ref, wq_ref, wk_ref, wv_ref, wu_ref, wg_ref,
                     q_ref, k_ref, v_ref, u_ref, sg_ref):
    xb = _norm_mod(x_ref[...], g_ref[...], mod_ref[0:1, :], mod_ref[1:2, :]).astype(BF16)
    cos = cos_ref[...]
    sin = sin_ref[...]
    lane = lax.broadcasted_iota(jnp.int32, (1, LANES), 1)
    low = (lane & 31) < 16
    _rope_store(_dot(xb, wq_ref[...]), cos, sin, low, ATT_Q_SCALE, q_ref)
    _rope_store(_dot(xb, wk_ref[...]), cos, sin, low, 1.0, k_ref)
    v_ref[...] = _dot(xb, wv_ref[...]).astype(v_ref.dtype)
    u_ref[...] = _dot(xb, wu_ref[...])
    sg_ref[...] = _sigmoid(_dot(xb, wg_ref[...])).astype(sg_ref.dtype)


def _inproj_x(x, mod, g, cos, sin, wq, wk, wv, wu, wg, tm):
    b, l, d = x.shape
    tok = lambda n: pl.BlockSpec((None, tm, n), lambda bi, i: (bi, i, 0))
    return pl.pallas_call(
        _inproj_x_kernel,
        grid=(b, l // tm),
        in_specs=[tok(d),
                  pl.BlockSpec((None, 2, d), lambda bi, i: (bi, 0, 0)),
                  pl.BlockSpec((1, d), lambda bi, i: (0, 0)),
                  pl.BlockSpec((tm, LANES), lambda bi, i: (i, 0)),
                  pl.BlockSpec((tm, LANES), lambda bi, i: (i, 0)),
                  _const_spec(wq.shape), _const_spec(wk.shape), _const_spec(wv.shape),
                  _const_spec(wu.shape), _const_spec(wg.shape)],
        out_specs=[tok(d), tok(d), tok(d), tok(SSM_W), tok(2 * d)],
        out_shape=[jax.ShapeDtypeStruct((b, l, d), BF16),
                   jax.ShapeDtypeStruct((b, l, d), BF16),
                   jax.ShapeDtypeStruct((b, l, d), BF16),
                   jax.ShapeDtypeStruct((b, l, SSM_W), F32),
                   jax.ShapeDtypeStruct((b, l, 2 * d), BF16)],
        compiler_params=_params(("arbitrary", "arbitrary")),
        name="inproj_x",
    )(x, mod, g, cos, sin, wq, wk, wv, wu, wg)


def _inproj_c_kernel(x_ref, mod_ref, g_ref, wk_ref, wv_ref, wu_ref, k_ref, v_ref, u_ref):
    xb = _norm_mod(x_ref[...], g_ref[...], mod_ref[0:1, :], mod_ref[1:2, :]).astype(BF16)
    k_ref[...] = _dot(xb, wk_ref[...]).astype(k_ref.dtype)
    v_ref[...] = _dot(xb, wv_ref[...]).astype(v_ref.dtype)
    u_ref[...] = _dot(xb, wu_ref[...])


def _inproj_c(ctx, mod, g, wk, wv, wu):
    b, n, d = ctx.shape
    tok = lambda w: pl.BlockSpec((None, n, w), lambda bi: (bi, 0, 0))
    return pl.pallas_call(
        _inproj_c_kernel,
        grid=(b,),
        in_specs=[tok(d),
                  pl.BlockSpec((2, d), lambda bi: (0, 0)),
                  pl.BlockSpec((1, d), lambda bi: (0, 0)),
                  _const_spec(wk.shape), _const_spec(wv.shape), _const_spec(wu.shape)],
        out_specs=[tok(d), tok(d), tok(SSM_W)],
        out_shape=[jax.ShapeDtypeStruct((b, n, d), BF16),
                   jax.ShapeDtypeStruct((b, n, d), BF16),
                   jax.ShapeDtypeStruct((b, n, SSM_W), F32)],
        compiler_params=_params(("arbitrary",)),
        name="inproj_ctx",
    )(ctx, mod, g, wk, wv, wu)


def _attn_chunks(kc_ref, kx_ref, tk):
    nc, nx = kc_ref.shape[0], kx_ref.shape[0]
    return [(False, 0, nc, 0)] + [(True, j * tk, tk, nc + j * tk) for j in range(nx // tk)]


def _attn_step(q_ref, kc_ref, vc_ref, kx_ref, vx_ref, lq1_ref, lk1_ref, lq2_ref, lk2_ref, sg_ref, o_ref,
               s_new_ref, m_new_ref, s_old_ref, m_old_ref, *, tq, tk, score, finish):
    if score:
        q = q_ref[...]
        lane = lax.broadcasted_iota(jnp.int32, (1, LANES), 1)
        zero = jnp.zeros_like(q)
        qs = jnp.concatenate([jnp.where(lane < ATT_DK, q, zero), jnp.where(lane >= ATT_DK, q, zero)], axis=0)
        m_lane = jnp.full((2 * tq, LANES), NEG_INF, F32)
    if finish:
        m = jnp.broadcast_to(jnp.max(m_old_ref[...], axis=1, keepdims=True), (2 * tq, LANES))
        acc = jnp.zeros((2 * tq, 2 * ATT_DV), F32)
    for latent, off, n, col in _attn_chunks(kc_ref, kx_ref, tk):
        k_ref, v_ref = (kx_ref, vx_ref) if latent else (kc_ref, vc_ref)
        if score:
            s = _dot_nt(qs, k_ref[off:off + n, :])
            s_new_ref[:, col:col + n] = s
            smax = s[:, 0:LANES]
            for g in range(1, n // LANES):
                smax = jnp.maximum(smax, s[:, g * LANES:(g + 1) * LANES])
            m_lane = jnp.maximum(m_lane, smax)
        if finish:
            mc = m
            if score:
                mc = m + jnp.minimum(jnp.abs(smax), 0.0)
            ps = [jnp.exp2(s_old_ref[:, col + g * LANES:col + (g + 1) * LANES] - mc).astype(BF16)
                  for g in range(n // LANES)]
            v1 = jnp.concatenate([v_ref[off:off + n, :], jnp.ones((n, ATT_DV), BF16)], axis=1)
            acc = acc + _dot(jnp.concatenate(ps, axis=1), v1)
    if score:
        m_new_ref[...] = m_lane
    if finish:
        o = acc[:, :ATT_DV] / acc[:, ATT_DV:]
        lam = (jnp.exp(jnp.sum(lq1_ref[...] * lk1_ref[...], axis=1, keepdims=True))
               - jnp.exp(jnp.sum(lq2_ref[...] * lk2_ref[...], axis=1, keepdims=True)) + LAMBDA_INIT)
        o = o[:tq] - lam * o[tq:]
        ms = jnp.mean(o * o, axis=-1, keepdims=True)
        o = o * lax.rsqrt(ms + NORM_EPS) * sg_ref[...] * (1.0 - LAMBDA_INIT)
        o_ref[...] = o.astype(o_ref.dtype)


def _attn_kernel(q_ref, kc_ref, vc_ref, kx_ref, vx_ref, lq1_ref, lk1_ref, lq2_ref, lk2_ref, sg_ref, o_ref,
                 s0_ref, s1_ref, m0_ref, m1_ref, *, tq, tk):
    i = pl.program_id(2)
    step = functools.partial(_attn_step, q_ref, kc_ref, vc_ref, kx_ref, vx_ref, lq1_ref, lk1_ref, lq2_ref, lk2_ref,
                             sg_ref, o_ref, tq=tq, tk=tk)

    @pl.when(i == 0)
    def _():
        step(s0_ref, m0_ref, s1_ref, m1_ref, score=True, finish=False)

    last = pl.num_programs(2) - 1

    @pl.when(jnp.logical_and(i % 2 == 1, i < last))
    def _():
        step(s1_ref, m1_ref, s0_ref, m0_ref, score=True, finish=True)

    @pl.when(jnp.logical_and(jnp.logical_and(i > 0, i % 2 == 0), i < last))
    def _():
        step(s0_ref, m0_ref, s1_ref, m1_ref, score=True, finish=True)

    @pl.when(jnp.logical_and(i % 2 == 1, i == last))
    def _():
        step(s1_ref, m1_ref, s0_ref, m0_ref, score=False, finish=True)

    @pl.when(jnp.logical_and(jnp.logical_and(i > 0, i % 2 == 0), i == last))
    def _():
        step(s0_ref, m0_ref, s1_ref, m1_ref, score=False, finish=True)


def _attention(q, kc, vc, kx, vx, lq1, lk1, lq2, lk2, subln_g, tq, tk):
    b, l, d = q.shape
    nc = kc.shape[1]
    nq = l // tq
    vec = lambda n: pl.BlockSpec((1, n), lambda bi, h, i: (0, 0))
    return pl.pallas_call(
        functools.partial(_attn_kernel, tq=tq, tk=tk),
        grid=(b, ATT_HEADS, nq + 1),
        in_specs=[pl.BlockSpec((None, tq, LANES), lambda bi, h, i: (bi, jnp.minimum(i, nq - 1), h)),
                  pl.BlockSpec((None, nc, LANES), lambda bi, h, i: (bi, 0, h)),
                  pl.BlockSpec((None, nc, LANES), lambda bi, h, i: (bi, 0, h)),
                  pl.BlockSpec((None, l, LANES), lambda bi, h, i: (bi, 0, h)),
                  pl.BlockSpec((None, l, LANES), lambda bi, h, i: (bi, 0, h)),
                  vec(ATT_DK), vec(ATT_DK), vec(ATT_DK), vec(ATT_DK), vec(ATT_DV)],
        out_specs=pl.BlockSpec((None, tq, LANES), lambda bi, h, i: (bi, jnp.maximum(i - 1, 0), h)),
        out_shape=jax.ShapeDtypeStruct((b, l, d), BF16),
        scratch_shapes=[pltpu.VMEM((2 * tq, nc + l), F32), pltpu.VMEM((2 * tq, nc + l), F32),
                        pltpu.VMEM((2 * tq, LANES), F32), pltpu.VMEM((2 * tq, LANES), F32)],
        compiler_params=_params(("arbitrary", "arbitrary", "arbitrary")),
        name="diff_attn",
    )(q, kc, vc, kx, vx, lq1, lk1, lq2, lk2, subln_g)


def _ssm_weights(a_re, a_im, log_dt, b_re, b_im, c_re, c_im, d_skip, nsteps):
    ch, g, n, p = SSM_CHUNK, SSM_GROUPS, SSM_STATE, SSM_GROUP
    ar = a_re.astype(F32)
    ai = a_im.astype(F32)
    dt = jnp.exp(log_dt.astype(F32))[..., None]
    lam = ar * dt
    theta = ai * dt
    mag = jnp.exp(lam)
    abar_re = mag * jnp.cos(theta)
    abar_im = mag * jnp.sin(theta)
    den = ar * ar + ai * ai
    nr = abar_re - 1.0
    ni = abar_im
    fr = (nr * ar + ni * ai) / den
    fi = (ni * ar - nr * ai) / den
    br = b_re.astype(F32)
    bi = b_im.astype(F32)
    bb_re = fr[..., None] * br - fi[..., None] * bi
    bb_im = fr[..., None] * bi + fi[..., None] * br
    cr = c_re.astype(F32)
    ci = c_im.astype(F32)

    j = jnp.arange(ch + 1, dtype=F32)[:, None, None, None]
    pr = jnp.exp(lam[None] * j) * jnp.cos(theta[None] * j)
    pi = jnp.exp(lam[None] * j) * jnp.sin(theta[None] * j)

    cb_re = cr[..., None] * bb_re[:, :, None] - ci[..., None] * bb_im[:, :, None]
    cb_im = cr[..., None] * bb_im[:, :, None] + ci[..., None] * bb_re[:, :, None]
    kker = (jnp.einsum("dgpnq,ldgn->dlgpq", cb_re, pr[:ch], precision=HIGHEST)
            - jnp.einsum("dgpnq,ldgn->dlgpq", cb_im, pi[:ch], precision=HIGHEST))
    ii = jnp.arange(ch)[:, None]
    jj = jnp.arange(ch)[None, :]
    kf = jnp.where((jj >= ii)[..., None, None, None], kker[0][jnp.clip(jj - ii, 0, ch - 1)], 0.0)
    kb = jnp.where((ii >= jj)[..., None, None, None], kker[1][jnp.clip(ii - jj, 0, ch - 1)], 0.0)
    mm = jnp.transpose(kf + kb, (2, 0, 4, 1, 3)).reshape(g, ch * p, ch * p)
    dvec = jnp.tile(d_skip.astype(F32).reshape(g, 1, p), (1, ch, 1)).reshape(g, ch * p)
    mm = mm + jnp.eye(ch * p, dtype=F32)[None] * dvec[:, None, :]

    def bpow(d, idx):
        pre = pr[idx, d][:, :, :, None]
        pim = pi[idx, d][:, :, :, None]
        re = pre * bb_re[d][None] - pim * bb_im[d][None]
        im = pre * bb_im[d][None] + pim * bb_re[d][None]
        both = jnp.stack([re, im], axis=0)
        return jnp.transpose(both, (2, 1, 4, 0, 3)).reshape(g, ch * p, 2 * n)

    def cpow(d, idx):
        pre = pr[idx, d][:, :, None, :]
        pim = pi[idx, d][:, :, None, :]
        ca_re = cr[d][None] * pre - ci[d][None] * pim
        ca_im = cr[d][None] * pim + ci[d][None] * pre
        both = jnp.stack([ca_re, -ca_im], axis=0)
        return jnp.transpose(both, (2, 0, 4, 1, 3)).reshape(g, 2 * n, ch * p)

    ar_idx = jnp.arange(ch)
    bpf = bpow(0, ch - 1 - ar_idx)
    bpb = bpow(1, ar_idx)
    cpf = cpow(0, ar_idx + 1)
    cpb = cpow(1, ch - ar_idx)

    re, im = pr[ch], pi[ch]
    rows1, rows2 = [], []
    for _ in range(nsteps):
        rows1.append(jnp.concatenate([re, re], axis=-1))
        rows2.append(jnp.concatenate([-im, im], axis=-1))
        re, im = re * re - im * im, 2.0 * re * im
    pad = [jnp.zeros_like(rows1[0])] * (16 - nsteps)
    coef = jnp.stack([jnp.stack(rows1 + pad, axis=2), jnp.stack(rows2 + pad, axis=2)], axis=2)
    coef = jnp.transpose(coef, (1, 0, 2, 3, 4)).reshape(g, 4, 16, 2 * n)
    return mm, bpf, bpb, cpf, cpb, coef


def _ssm_kernel(ux_ref, uc_ref, m_ref, bpf_ref, bpb_ref, cpf_ref, cpb_ref, coef_ref, y_ref, *, nb, nsteps):
    ux = ux_ref[...]
    uc = uc_ref[...]
    rx, rc = ux.shape[0], uc.shape[0]
    r = rx + rc
    bpf = bpf_ref[...]
    bpb = bpb_ref[...]
    hf = jnp.concatenate([_dot(uc, bpf, HIGHEST), _dot(ux, bpf, HIGHEST)], axis=0)
    gb = jnp.concatenate([_dot(ux, bpb, HIGHEST), _dot(uc, bpb, HIGHEST)], axis=0)
    row = lax.broadcasted_iota(jnp.int32, (r, 1), 0)
    half = SSM_STATE
    for k in range(nsteps):
        s = nb << k
        if s >= r:
            break
        sh = jnp.where(row >= s, pltpu.roll(hf, s, 0), 0.0)
        hf = hf + coef_ref[0, k:k + 1, :] * sh + coef_ref[1, k:k + 1, :] * pltpu.roll(sh, half, 1)
        sh = jnp.where(row < r - s, pltpu.roll(gb, r - s, 0), 0.0)
        gb = gb + coef_ref[2, k:k + 1, :] * sh + coef_ref[3, k:k + 1, :] * pltpu.roll(sh, half, 1)
    hprev = jnp.where(row >= nb, pltpu.roll(hf, nb, 0), 0.0)[rc:]
    gnext = jnp.where(row < r - nb, pltpu.roll(gb, r - nb, 0), 0.0)[:rx]
    y_ref[...] = (_dot(ux, m_ref[...], HIGHEST) + _dot(hprev, cpf_ref[...], HIGHEST)
                  + _dot(gnext, cpb_ref[...], HIGHEST))


def _ssm(ux, uc, weights, nb, nsteps):
    g, rx, w = ux.shape
    rc = uc.shape[1]
    mm, bpf, bpb, cpf, cpb, coef = weights
    per_g = lambda a: pl.BlockSpec((None,) + a.shape[1:], lambda gi: (gi,) + (0,) * (a.ndim - 1))
    return pl.pallas_call(
        functools.partial(_ssm_kernel, nb=nb, nsteps=nsteps),
        grid=(g,),
        in_specs=[per_g(ux), per_g(uc), per_g(mm), per_g(bpf), per_g(bpb), per_g(cpf), per_g(cpb), per_g(coef)],
        out_specs=pl.BlockSpec((None, rx, w), lambda gi: (gi, 0, 0)),
        out_shape=jax.ShapeDtypeStruct((g, rx, w), F32),
        compiler_params=_params(("arbitrary",)),
        name="s5_chunked",
    )(ux, uc, mm, bpf, bpb, cpf, cpb, coef)


def _merge_kernel(x_ref, attn_ref, y_ref, sg_ref, mod_ref, g2_ref, wglu_ref, bglu_ref, wau_ref, wsu_ref, wout_ref,
                  wqry_ref, k1_ref, k2_ref, x1_ref, xn_ref, s1_ref, s2_ref):
    d = D_MODEL
    yg = _gelu_tanh(y_ref[...])
    ysg = yg * _sigmoid(_dot(yg.astype(BF16), wglu_ref[...]) + bglu_ref[...])
    a_up = _dot(attn_ref[...], wau_ref[...])
    s_up = _dot(ysg.astype(BF16), wsu_ref[...])
    mix_in = sg_ref[:, :d].astype(F32) * a_up + sg_ref[:, d:].astype(F32) * s_up
    mix = _dot(mix_in.astype(BF16), wout_ref[...])
    x1 = x_ref[...] + mod_ref[0:1, :] * mix
    x1_ref[...] = x1
    xn = _norm_mod(x1, g2_ref[...], mod_ref[1:2, :], mod_ref[2:3, :]).astype(BF16)
    xn_ref[...] = xn
    qb = _dot(xn, wqry_ref[...]).astype(BF16)
    for h in range(PEER_HEADS):
        qh = qb[:, h * LANES:(h + 1) * LANES]
        s1_ref[h * LANES:(h + 1) * LANES, :] = _dot_nt(k1_ref[...], qh)
        s2_ref[h * LANES:(h + 1) * LANES, :] = _dot_nt(k2_ref[...], qh)


def _merge(x, attn, y, sg, mod, g2, wglu, bglu, wau, wsu, wout, wqry, k1p, k2p, tm):
    b, l, d = x.shape
    ntok = b * l
    nt = l // tm
    tok = lambda n: pl.BlockSpec((None, tm, n), lambda bi, i: (bi, i, 0))
    tokT = pl.BlockSpec((PEER_HEADS * PEER_NKEYS, tm), lambda bi, i: (0, bi * nt + i))
    return pl.pallas_call(
        _merge_kernel,
        grid=(b, nt),
        in_specs=[tok(d), tok(d), tok(SSM_W), tok(2 * d),
                  pl.BlockSpec((None, 8, d), lambda bi, i: (bi, 0, 0)),
                  pl.BlockSpec((1, d), lambda bi, i: (0, 0)),
                  _const_spec(wglu.shape), _const_spec(bglu.shape), _const_spec(wau.shape),
                  _const_spec(wsu.shape), _const_spec(wout.shape), _const_spec(wqry.shape),
                  _const_spec(k1p.shape), _const_spec(k2p.shape)],
        out_specs=[tok(d), tok(d), tokT, tokT],
        out_shape=[jax.ShapeDtypeStruct((b, l, d), F32),
                   jax.ShapeDtypeStruct((b, l, d), BF16),
                   jax.ShapeDtypeStruct((PEER_HEADS * PEER_NKEYS, ntok), F32),
                   jax.ShapeDtypeStruct((PEER_HEADS * PEER_NKEYS, ntok), F32)],
        compiler_params=_params(("arbitrary", "arbitrary")),
        name="merge_peer_query",
    )(x, attn, y, sg, mod, g2, wglu, bglu, wau, wsu, wout, wqry, k1p, k2p)


CAND_ROWS = 80


def _pair_candidates(v1, v2):
    rows8 = lax.broadcasted_iota(jnp.int32, (8, 1), 0)
    blocks = [v1[0:1, :] + v2]
    for a in range(1, 8):
        blocks.append(jnp.where(rows8 < PEER_TOPK // (a + 1), v1[a:a + 1, :] + v2[0:8, :], NEG_INF))
    blocks.append(v1[8:16, :] + v2[0:1, :])
    return jnp.concatenate(blocks, axis=0)


def _route_kernel(s1_ref, s2_ref, nsel_ref, coef_ref, rank2_ref, e2_ref,
                  c1_ref, c2_ref, rk_ref, v1_ref, v2_ref, cand_ref, tau_ref):
    nh, nk, topk = PEER_HEADS, PEER_NKEYS, PEER_TOPK
    keys = lambda h: slice(h * nk, (h + 1) * nk)
    tops = lambda h: slice(h * topk, (h + 1) * topk)
    cands = lambda h: slice(h * CAND_ROWS, (h + 1) * CAND_ROWS)
    c1_ref[...] = s1_ref[...]
    c2_ref[...] = s2_ref[...]
    rk_ref[...] = jnp.full(rk_ref.shape, float(topk), F32)
    v1_ref[...] = jnp.zeros_like(v1_ref)
    v2_ref[...] = jnp.zeros_like(v2_ref)
    rows16 = lax.broadcasted_iota(jnp.int32, (topk, 1), 0)

    def extract(k, _):
        for h in range(nh):
            cur = c1_ref[keys(h), :]
            m = jnp.max(cur, axis=0, keepdims=True)
            c1_ref[keys(h), :] = jnp.where(cur >= m, NEG_INF, cur)
            v1_ref[tops(h), :] = jnp.where(rows16 == k, m, v1_ref[tops(h), :])
            cur = c2_ref[keys(h), :]
            m = jnp.max(cur, axis=0, keepdims=True)
            hit = cur >= m
            c2_ref[keys(h), :] = jnp.where(hit, NEG_INF, cur)
            rk_ref[keys(h), :] = jnp.where(hit, k.astype(F32), rk_ref[keys(h), :])
            v2_ref[tops(h), :] = jnp.where(rows16 == k, m, v2_ref[tops(h), :])
        return 0

    lax.fori_loop(0, topk, extract, 0)

    for h in range(nh):
        cand_ref[cands(h), :] = _pair_candidates(v1_ref[tops(h), :], v2_ref[tops(h), :])

    def knock(k, _):
        for h in range(nh):
            cur = cand_ref[cands(h), :]
            m = jnp.max(cur, axis=0, keepdims=True)
            cand_ref[cands(h), :] = jnp.where(cur >= m, NEG_INF, cur)
            tau_ref[h:h + 1, :] = m
        return 0

    lax.fori_loop(0, topk, knock, 0)

    for h in range(nh):
        v1 = v1_ref[tops(h), :]
        v2 = v2_ref[tops(h), :]
        tau = tau_ref[h:h + 1, :]
        cand = _pair_candidates(v1, v2)
        mx = v1[0:1, :] + v2[0:1, :]
        z = jnp.sum(jnp.where(cand >= tau, jnp.exp(cand - mx), 0.0), axis=0, keepdims=True)
        s1 = s1_ref[keys(h), :]
        nsel = jnp.zeros_like(s1)
        for bpos in range(topk):
            nsel = nsel + jnp.where(s1 + v2[bpos:bpos + 1, :] >= tau, 1.0, 0.0)
        nsel_ref[keys(h), :] = jnp.where(s1 >= v1[topk - 1:topk, :], nsel, 0.0)
        coef_ref[keys(h), :] = jnp.exp(s1 - v1[0:1, :]) / z
        rank2_ref[keys(h), :] = rk_ref[keys(h), :].astype(rank2_ref.dtype)
        e2_ref[keys(h), :] = jnp.exp(s2_ref[keys(h), :] - v2[0:1, :]).astype(e2_ref.dtype)


def _route(s1t, s2t, tt):
    rows, ntok = s1t.shape
    spec = pl.BlockSpec((rows, tt), lambda i: (0, i))
    f32 = jax.ShapeDtypeStruct((rows, ntok), F32)
    bf16 = jax.ShapeDtypeStruct((rows, ntok), BF16)
    big = pltpu.VMEM((rows, tt), F32)
    small = pltpu.VMEM((PEER_HEADS * PEER_TOPK, tt), F32)
    return pl.pallas_call(
        _route_kernel,
        grid=(ntok // tt,),
        in_specs=[spec, spec],
        out_specs=[spec, spec, spec, spec],
        out_shape=[f32, f32, bf16, bf16],
        scratch_shapes=[big, big, big, small, small,
                        pltpu.VMEM((PEER_HEADS * CAND_ROWS, tt), F32), pltpu.VMEM((PEER_HEADS, tt), F32)],
        compiler_params=_params(("arbitrary",)),
        name="peer_route",
    )(s1t, s2t)


def _peer_step(xn_ref, u_ref, vt_ref, rank2_ref, e2_ref, nsel_ref, coef_ref, act_ref, gate_ref, acc_ref, *, te, tt):
    act_ref[...] = _dot_nt(u_ref[...], xn_ref[...]).astype(BF16)
    nsub = te // PEER_NKEYS
    zero = jnp.zeros((PEER_NKEYS, LANES), BF16)
    for lb in range(tt // LANES):
        ls = slice(lb * LANES, (lb + 1) * LANES)
        ns = [nsel_ref[h, :, ls].astype(BF16) for h in range(PEER_HEADS)]
        cf = [coef_ref[h, :, ls].astype(BF16) for h in range(PEER_HEADS)]
        for r in range(0, nsub, 2):
            w = [zero, zero]
            for h in range(PEER_HEADS):
                hs = slice(h * PEER_NKEYS, (h + 1) * PEER_NKEYS)
                rk = rank2_ref[hs, ls]
                ee = e2_ref[hs, ls]
                for k in range(2):
                    w[k] = w[k] + jnp.where(rk < ns[h][r + k:r + k + 1, :], ee, zero) * cf[h][r + k:r + k + 1, :]
            for k in range(2):
                rs = slice((r + k) * PEER_NKEYS, (r + k + 1) * PEER_NKEYS)
                gate_ref[rs, ls] = (w[k].astype(F32) * _gelu_tanh(act_ref[rs, ls].astype(F32))).astype(BF16)
    acc_ref[...] += _dot(vt_ref[...], gate_ref[...])


def _peer_kernel(xn_ref, u_ref, vt_ref, rank2_ref, e2_ref, nsel_ref, coef_ref, o_ref,
                 act_ref, gate_ref, rank2b_ref, e2b_ref, *, te, tt):
    @pl.when(pl.program_id(2) == 0)
    def _():
        o_ref[...] = jnp.zeros_like(o_ref)
        rank2b_ref[...] = rank2_ref[...]
        e2b_ref[...] = e2_ref[...]

    _peer_step(xn_ref, u_ref, vt_ref, rank2b_ref, e2b_ref, nsel_ref, coef_ref, act_ref, gate_ref, o_ref, te=te, tt=tt)


def _peer(xn, u, vt, rank2, e2t, nsel, coef, tt, te):
    b, l, d = xn.shape
    nt = l // tt
    ne = PEER_EXPERTS // te
    nsub = te // PEER_NKEYS
    rows = PEER_HEADS * PEER_NKEYS
    ntok = b * l
    once = dict(pipeline_mode=pl.Buffered(1))
    tokT = pl.BlockSpec((rows, tt), lambda bi, i, j: (0, bi * nt + i), **once)
    chunk_rows = pl.BlockSpec((PEER_HEADS, None, nsub, tt), lambda bi, i, j: (0, j, 0, bi * nt + i))
    split = lambda a: a.reshape(PEER_HEADS, PEER_NKEYS // nsub, nsub, ntok)
    return pl.pallas_call(
        functools.partial(_peer_kernel, te=te, tt=tt),
        grid=(b, nt, ne),
        in_specs=[pl.BlockSpec((None, tt, d), lambda bi, i, j: (bi, i, 0), **once),
                  pl.BlockSpec((te, d), lambda bi, i, j: (j, 0)),
                  pl.BlockSpec((d, te), lambda bi, i, j: (0, j)),
                  tokT, tokT, chunk_rows, chunk_rows],
        out_specs=pl.BlockSpec((d, tt), lambda bi, i, j: (0, bi * nt + i)),
        out_shape=jax.ShapeDtypeStruct((d, ntok), F32),
        scratch_shapes=[pltpu.VMEM((te, tt), BF16), pltpu.VMEM((te, tt), BF16),
                        pltpu.VMEM((rows, tt), BF16), pltpu.VMEM((rows, tt), BF16)],
        compiler_params=_params(("arbitrary", "arbitrary", "arbitrary")),
        name="peer_dense",
    )(xn, u, vt, rank2, e2t, split(nsel), split(coef))


def _final_kernel(x1_ref, pt_ref, mod_ref, gf_ref, o_ref):
    x2 = x1_ref[...] + mod_ref[0:1, :] * pt_ref[...].T
    ms = jnp.mean(x2 * x2, axis=-1, keepdims=True)
    o_ref[...] = x2 * lax.rsqrt(ms + NORM_EPS) * gf_ref[...]


def _final(x1, peer_t, mod, gf, tm):
    b, l, d = x1.shape
    nt = l // tm
    tok = pl.BlockSpec((None, tm, d), lambda bi, i: (bi, i, 0))
    return pl.pallas_call(
        _final_kernel,
        grid=(b, nt),
        in_specs=[tok,
                  pl.BlockSpec((d, tm), lambda bi, i: (0, bi * nt + i)),
                  pl.BlockSpec((None, 8, d), lambda bi, i: (bi, 0, 0)),
                  pl.BlockSpec((1, d), lambda bi, i: (0, 0))],
        out_specs=tok,
        out_shape=jax.ShapeDtypeStruct((b, l, d), F32),
        compiler_params=_params(("arbitrary", "arbitrary")),
        name="final_norm",
    )(x1, peer_t, mod, gf)


def _rope_tables(l):
    n_freq = ATT_DK // 4
    inv = ROPE_BASE ** (-2.0 * jnp.arange(n_freq, dtype=F32) / (ATT_DK // 2))
    t = jnp.arange(l, dtype=jnp.int32)
    ang_r = (t // GRID_W).astype(F32)[:, None] * inv[None, :]
    ang_c = (t % GRID_W).astype(F32)[:, None] * inv[None, :]
    cos = jnp.concatenate([jnp.cos(ang_r)] * 2 + [jnp.cos(ang_c)] * 2, axis=-1)
    sin = jnp.concatenate([-jnp.sin(ang_r), jnp.sin(ang_r), -jnp.sin(ang_c), jnp.sin(ang_c)], axis=-1)
    return jnp.tile(cos, (1, 2)), jnp.tile(sin, (1, 2))


def _pick(n, pref):
    for t in pref:
        if n % t == 0:
            return t
    return n


def kernel(x, c, ctx, c_ctx, ada_w, ada_b, norm1_g, norm2_g, w_in, lambda_q1, lambda_k1, lambda_q2, lambda_k2, subln_g, w_attn_up, ssm_a_re, ssm_a_im, ssm_log_dt, ssm_b_re, ssm_b_im, ssm_c_re, ssm_c_im, ssm_d, w_glu, b_glu, w_ssm_up, w_out, peer_w_query, peer_sub_k1, peer_sub_k2, peer_u, peer_v, final_norm_g):
    assert ada_w.shape[0] == 1, "single-layer block"
    b, l, d = x.shape
    nc = ctx.shape[1]
    assert d == D_MODEL and b + 1 <= 8
    ch = SSM_CHUNK

    cin = jnp.concatenate([c, c_ctx[None, :], jnp.zeros((7 - b, d), F32)], axis=0)
    mod = _adaln(cin, ada_w[0], ada_b[0][None, :])
    sh1, sc1, g1, sh2, sc2, g2 = [mod[:, k * d:(k + 1) * d] for k in range(6)]
    mod1 = jnp.stack([sh1[:b], sc1[:b]], axis=1)
    mod1c = jnp.stack([sh1[b], sc1[b]], axis=0)
    zpad = jnp.zeros((b, 5, d), F32)
    mod2 = jnp.concatenate([jnp.stack([g1[:b], sh2[:b], sc2[:b]], axis=1), zpad], axis=1)
    mod3 = jnp.concatenate([g2[:b, None, :], jnp.zeros((b, 7, d), F32)], axis=1)

    w = w_in[0]
    n_q = 2 * ATT_HEADS * ATT_DK
    o_k, o_v, o_u, o_g = n_q, 2 * n_q, 2 * n_q + ATT_HEADS * ATT_DV, 2 * n_q + ATT_HEADS * ATT_DV + SSM_W
    wq, wk, wv, wu, wg = [w[:, a:e].astype(BF16) for a, e in
                          ((0, o_k), (o_k, o_v), (o_v, o_u), (o_u, o_g), (o_g, w.shape[1]))]
    cos, sin = _rope_tables(l)
    g1n = norm1_g[0][None, :]

    tm = _pick(l, (512, 256, 128))
    q, k, v, u, sg = _inproj_x(x, mod1, g1n, cos, sin, wq, wk, wv, wu, wg, tm)
    kc, vc, uc = _inproj_c(ctx, mod1c, g1n, wk, wv, wu)

    row = lambda a: a[0][None, :].astype(F32)
    attn = _attention(q, kc, vc, k, v, row(lambda_q1), row(lambda_k1), row(lambda_q2), row(lambda_k2),
                      row(subln_g), _pick(l, (256, 128)), _pick(l, (256, 128)))

    nsteps = max(1, math.ceil(math.log2((l + nc) // ch)))
    sw = _ssm_weights(ssm_a_re[0], ssm_a_im[0], ssm_log_dt[0], ssm_b_re[0], ssm_b_im[0],
                      ssm_c_re[0], ssm_c_im[0], ssm_d[0], nsteps)
    to_groups = lambda a, n: jnp.transpose(a.reshape(b, n // ch, ch, SSM_GROUPS, SSM_GROUP),
                                           (3, 1, 0, 2, 4)).reshape(SSM_GROUPS, (n // ch) * b, ch * SSM_GROUP)
    yg = _ssm(to_groups(u, l), to_groups(uc, nc), sw, b, nsteps)
    y = jnp.transpose(yg.reshape(SSM_GROUPS, l // ch, b, ch, SSM_GROUP), (2, 1, 3, 0, 4)).reshape(b, l, SSM_W)

    half = PEER_DKEY // 2
    zk = jnp.zeros((PEER_NKEYS, half), F32)
    k1p = jnp.concatenate([peer_sub_k1[0], zk], axis=1).astype(BF16)
    k2p = jnp.concatenate([zk, peer_sub_k2[0]], axis=1).astype(BF16)
    x1, xn2, s1t, s2t = _merge(x, attn, y, sg, mod2, norm2_g[0][None, :],
                               w_glu[0].astype(BF16), b_glu[0][None, :], w_attn_up[0].astype(BF16),
                               w_ssm_up[0].astype(BF16), w_out[0].astype(BF16), peer_w_query[0].astype(BF16),
                               k1p, k2p, tm)

    nsel, coef, rank2, e2t = _route(s1t, s2t, _pick(b * l, (256, 128)))
    peer_t = _peer(xn2, peer_u[0].astype(BF16), peer_v[0].T.astype(BF16), rank2, e2t, nsel, coef,
                   _pick(l, (1024, 512, 256, 128)), 1024)
    return _final(x1, peer_t, mod3, final_norm_g[None, :], tm)
```
